```python
import jax, jax.numpy as jnp
from jax import lax
import numpy as np

D_MODEL = 4096
BATCH = 32
SEQ = 256
DEPTH = 4
DEC_BATCH = 4
DEC_SEQ = 1024
PAST_LEN = 256

GRID_W = 64
HEAD_DIM = 128
Q_BLOCK = 128
ROPE_THETA = 10000.0
EPS = 1e-6
NEG_INF = -1e30
N_MOD = 6
N_EVEN = (DEPTH + 1) // 2
N_ODD = DEPTH // 2
A_HEADS = D_MODEL // (2 * HEAD_DIM)
KV_HEADS = A_HEADS // 4
A_GROUP = A_HEADS // KV_HEADS
A_WIDTH = A_HEADS * HEAD_DIM
KV_WIDTH = KV_HEADS * HEAD_DIM
B_WIDTH = D_MODEL // 2
CONV_W = 3
C_HEADS = D_MODEL // (2 * HEAD_DIM)
Q_LORA = D_MODEL // 4
KV_LORA = D_MODEL // 8
QK_NOPE = HEAD_DIM
QK_ROPE = HEAD_DIM // 2
V_DIM = HEAD_DIM
QK_HEAD = QK_NOPE + QK_ROPE
D_HEADS = D_MODEL // (2 * HEAD_DIM)
D_GROUP = D_HEADS // KV_HEADS
D_WIDTH = D_HEADS * HEAD_DIM
WINDOW = 128
EVEN_IN = A_WIDTH + 2 * KV_WIDTH + 3 * B_WIDTH
EVEN_OUT = A_WIDTH + B_WIDTH
ODD_IN = Q_LORA + KV_LORA + QK_ROPE + D_WIDTH + 2 * KV_WIDTH
ODD_OUT = C_HEADS * V_DIM + D_WIDTH
N_GROUPS = 4
EXPERTS_PER_GROUP = 4
N_EXPERTS = N_GROUPS * EXPERTS_PER_GROUP
TOP_K = 2
D_EXPERT = D_MODEL // 4
MOE_BLOCK = 128

kernel_name = 'hybrid_diffusion_prefix_step'


def split_last(x, sizes):
    return jnp.split(x, np.cumsum(sizes)[:-1].tolist(), axis=-1)


def rms_norm(x, g):
    xf = x.astype(jnp.float32)
    y = xf * lax.rsqrt(jnp.mean(xf * xf, axis=-1, keepdims=True) + EPS)
    return (y * g.astype(jnp.float32)).astype(x.dtype)


def adaln(x, g, shift, scale):
    return rms_norm(x, g) * (1 + scale) + shift


def modulation(cvec, w, b):
    m = jax.nn.silu(cvec) @ w + b
    return jnp.split(m[:, None, :], N_MOD, axis=-1)


def axial_rope_tables(n_tok, rot_dim):
    rows = n_tok // GRID_W
    row_id = jnp.repeat(jnp.arange(rows, dtype=jnp.float32), GRID_W)
    col_id = jnp.tile(jnp.arange(GRID_W, dtype=jnp.float32), rows)
    axis_dim = rot_dim // 2
    inv_freq = ROPE_THETA ** (-jnp.arange(0, axis_dim, 2, dtype=jnp.float32) / axis_dim)
    ang_r = row_id[:, None] * inv_freq[None, :]
    ang_c = col_id[:, None] * inv_freq[None, :]
    return (jnp.cos(ang_r), jnp.sin(ang_r), jnp.cos(ang_c), jnp.sin(ang_c))


def _rotate(x, cos, sin):
    half = x.shape[-1] // 2
    shape = (cos.shape[0],) + (1,) * (x.ndim - 3) + (cos.shape[1],)
    c = cos.reshape(shape).astype(x.dtype)
    s = sin.reshape(shape).astype(x.dtype)
    x1, x2 = x[..., :half], x[..., half:]
    return jnp.concatenate([x1 * c - x2 * s, x1 * s + x2 * c], axis=-1)


def apply_axial_rope(x, tables):
    cr, sr, cc, sc = tables
    a = x.shape[-1] // 2
    return jnp.concatenate([_rotate(x[..., :a], cr, sr), _rotate(x[..., a:], cc, sc)], axis=-1)


def rope_tail(x, tables):
    return jnp.concatenate([x[..., :QK_NOPE], apply_axial_rope(x[..., QK_NOPE:], tables)], axis=-1)


def blocked_attention(q, k, v, sink=None):
    bsz, sq, hk, g, dq = q.shape
    nb = sq // Q_BLOCK
    scale = dq ** -0.5
    qb = jnp.moveaxis(q.reshape(bsz, nb, Q_BLOCK, hk, g, dq), 1, 0)

    def attend(qblk):
        s = jnp.einsum('bqhgd,bkhd->bhgqk', qblk, k, preferred_element_type=jnp.float32) * scale
        if sink is None:
            p = jax.nn.softmax(s, axis=-1)
        else:
            col = jnp.broadcast_to(sink.astype(jnp.float32)[None, :, :, None, None], s.shape[:-1] + (1,))
            p = jax.nn.softmax(jnp.concatenate([s, col], axis=-1), axis=-1)[..., :-1]
        return jnp.einsum('bhgqk,bkhd->bqhgd', p.astype(v.dtype), v)

    o = lax.map(attend, qb)
    return jnp.moveaxis(o, 0, 1).reshape(bsz, sq, hk, g, v.shape[-1])


def banded_window_attention(q, k, v, k_ctx, v_ctx, sink):
    bsz, n, hk, g, dh = q.shape
    nb = n // WINDOW
    pad = ((0, 0), (WINDOW, WINDOW), (0, 0), (0, 0))
    kp = jnp.pad(k, pad).reshape(bsz, nb + 2, WINDOW, hk, dh)
    vp = jnp.pad(v, pad).reshape(bsz, nb + 2, WINDOW, hk, v.shape[-1])
    kw = jnp.concatenate([kp[:, :-2], kp[:, 1:-1], kp[:, 2:]], axis=2)
    vw = jnp.concatenate([vp[:, :-2], vp[:, 1:-1], vp[:, 2:]], axis=2)
    qb = q.reshape(bsz, nb, WINDOW, hk, g, dh)
    scale = dh ** -0.5
    s_loc = jnp.einsum('bnqhgd,bnkhd->bnhgqk', qb, kw, preferred_element_type=jnp.float32) * scale
    s_ctx = jnp.einsum('bnqhgd,bkhd->bnhgqk', qb, k_ctx, preferred_element_type=jnp.float32) * scale
    qpos = jnp.arange(n).reshape(nb, WINDOW)
    kpos = (jnp.arange(nb)[:, None] - 1) * WINDOW + jnp.arange(3 * WINDOW)[None, :]
    valid = ((jnp.abs(qpos[:, :, None] - kpos[:, None, :]) <= WINDOW)
             & (kpos[:, None, :] >= 0) & (kpos[:, None, :] < n))
    s_loc = jnp.where(valid[None, :, None, None], s_loc, NEG_INF)
    col = jnp.broadcast_to(sink.astype(jnp.float32)[None, None, :, :, None, None], s_loc.shape[:-1] + (1,))
    p = jax.nn.softmax(jnp.concatenate([s_loc, s_ctx, col], axis=-1), axis=-1)
    nl = 3 * WINDOW
    n_ctx = k_ctx.shape[1]
    p_loc = p[..., :nl].astype(v.dtype)
    p_ctx = p[..., nl:nl + n_ctx].astype(v.dtype)
    o = (jnp.einsum('bnhgqk,bnkhd->bnqhgd', p_loc, vw)
         + jnp.einsum('bnhgqk,bkhd->bnqhgd', p_ctx, v_ctx))
    return o.reshape(bsz, n, hk, g, v.shape[-1])


def short_conv_centered(u, w):
    up = jnp.pad(u, ((0, 0), (1, 1), (0, 0)))
    return up[:, :-2] * w[0] + up[:, 1:-1] * w[1] + up[:, 2:] * w[2]


def even_mixer(h, w_in, w_out, q_g, k_g, conv_w, ctx=None, rope=None):
    bsz, n, _ = h.shape
    q, k, v, u, gb, gc = split_last(h @ w_in, (A_WIDTH, KV_WIDTH, KV_WIDTH, B_WIDTH, B_WIDTH, B_WIDTH))
    q = rms_norm(q.reshape(bsz, n, KV_HEADS, A_GROUP, HEAD_DIM), q_g)
    k = rms_norm(k.reshape(bsz, n, KV_HEADS, HEAD_DIM), k_g)
    v = v.reshape(bsz, n, KV_HEADS, HEAD_DIM)
    if ctx is None:
        a = blocked_attention(q, k, v)
    else:
        ctx_k, ctx_v = ctx
        a = blocked_attention(apply_axial_rope(q, rope),
                              jnp.concatenate([apply_axial_rope(k, rope), ctx_k], axis=1),
                              jnp.concatenate([v, ctx_v], axis=1))
    b = gb * short_conv_centered(gc * u, conv_w)
    out = jnp.concatenate([a.reshape(bsz, n, A_WIDTH), b], axis=-1) @ w_out
    return out, (k, v)


def mla_keys_values(ckv, kpe, w_kvb, k_g):
    bsz, n, _ = ckv.shape
    kv = (ckv @ w_kvb).reshape(bsz, n, C_HEADS, QK_NOPE + V_DIM)
    k_nope, v = kv[..., :QK_NOPE], kv[..., QK_NOPE:]
    k_pe = jnp.broadcast_to(kpe[:, :, None, :], (bsz, n, C_HEADS, QK_ROPE))
    k = rms_norm(jnp.concatenate([k_nope, k_pe], axis=-1), k_g)
    return k, v


def odd_mixer(h, w_in, w_out, qa_g, kva_g, w_qb, w_kvb, cq_g, ck_g, dq_g, dk_g, sink,
              ctx=None, rope_hd=None, rope_pe=None):
    bsz, n, _ = h.shape
    cq, ckv, kpe, dq, dk, dv = split_last(
        h @ w_in, (Q_LORA, KV_LORA, QK_ROPE, D_WIDTH, KV_WIDTH, KV_WIDTH))
    ckv = rms_norm(ckv, kva_g)
    qc = rms_norm((rms_norm(cq, qa_g) @ w_qb).reshape(bsz, n, C_HEADS, 1, QK_HEAD), cq_g)
    kc, vc = mla_keys_values(ckv, kpe, w_kvb, ck_g)
    qd = rms_norm(dq.reshape(bsz, n, KV_HEADS, D_GROUP, HEAD_DIM), dq_g)
    kd = rms_norm(dk.reshape(bsz, n, KV_HEADS, HEAD_DIM), dk_g)
    vd = dv.reshape(bsz, n, KV_HEADS, HEAD_DIM)
    sink_hg = sink.reshape(KV_HEADS, D_GROUP)
    if ctx is None:
        oc = blocked_attention(qc, kc, vc)
        od = blocked_attention(qd, kd, vd, sink_hg)
    else:
        ctx_ckv, ctx_kpe, ctx_kd, ctx_vd = ctx
        kc_ctx, vc_ctx = mla_keys_values(ctx_ckv, ctx_kpe, w_kvb, ck_g)
        oc = blocked_attention(rope_tail(qc, rope_pe),
                               jnp.concatenate([rope_tail(kc, rope_pe), kc_ctx], axis=1),
                               jnp.concatenate([vc, vc_ctx], axis=1))
        od = banded_window_attention(apply_axial_rope(qd, rope_hd), apply_axial_rope(kd, rope_hd),
                                     vd, ctx_kd, ctx_vd, sink_hg)
    out = jnp.concatenate([oc.reshape(bsz, n, C_HEADS * V_DIM), od.reshape(bsz, n, D_WIDTH)], axis=-1) @ w_out
    return out, (ckv, kpe, kd, vd)


def moe_ffn(h, w_gr, b_gr, w_er, b_er, w_g, w_u, w_d):
    bsz, n, d = h.shape
    t = bsz * n
    xt = h.reshape(t, d)
    g_logits = (xt @ w_gr).astype(jnp.float32) + b_gr.astype(jnp.float32)
    g_prob = jax.nn.softmax(g_logits, axis=-1)
    g_idx = jnp.argmax(g_logits, axis=-1)
    e_logits = ((xt @ w_er).astype(jnp.float32) + b_er.astype(jnp.float32)).reshape(t, N_GROUPS, EXPERTS_PER_GROUP)
    e_in = jnp.take_along_axis(e_logits, g_idx[:, None, None], axis=1)[:, 0]
    top_p, top_i = lax.top_k(jax.nn.softmax(e_in, axis=-1), TOP_K)
    gate = jnp.take_along_axis(g_prob, g_idx[:, None], axis=1) * top_p / jnp.sum(top_p, axis=-1, keepdims=True)
    eid = (g_idx[:, None] * EXPERTS_PER_GROUP + top_i).reshape(-1).astype(jnp.int32)
    n_pairs = t * TOP_K
    order = jnp.argsort(eid)
    sorted_e = eid[order]
    counts = jnp.bincount(eid, length=N_EXPERTS)
    padded = (counts + MOE_BLOCK - 1) // MOE_BLOCK * MOE_BLOCK
    pad_end = jnp.cumsum(padded)
    pad_start = pad_end - padded
    start = jnp.cumsum(counts) - counts
    dest_sorted = pad_start[sorted_e] + jnp.arange(n_pairs) - start[sorted_e]
    n_blocks = (n_pairs + N_EXPERTS * (MOE_BLOCK - 1) + MOE_BLOCK - 1) // MOE_BLOCK
    n_slots = n_blocks * MOE_BLOCK
    slot_token = jnp.full((n_slots,), t, jnp.int32).at[dest_sorted].set((order // TOP_K).astype(jnp.int32))
    x_pad = jnp.concatenate([xt, jnp.zeros((1, d), xt.dtype)], axis=0)
    xb = x_pad[slot_token].reshape(n_blocks, MOE_BLOCK, d)
    block_e = jnp.minimum(jnp.searchsorted(pad_end, jnp.arange(n_blocks) * MOE_BLOCK, side='right'), N_EXPERTS - 1)

    def expert_block(args):
        xblk, e = args
        return (jax.nn.silu(xblk @ w_g[e]) * (xblk @ w_u[e])) @ w_d[e]

    yb = lax.map(expert_block, (xb, block_e)).reshape(n_slots, d)
    dest = jnp.zeros((n_pairs,), dest_sorted.dtype).at[order].set(dest_sorted)
    y = yb[dest].reshape(t, TOP_K, d)
    return jnp.einsum('tk,tkd->td', gate.astype(y.dtype), y).reshape(bsz, n, d)


def setup_inputs(seed: int = 0) -> dict:
    key = jax.random.key(seed)
    ks = iter(jax.random.split(key, 40))
    f32 = jnp.float32
    nrm = lambda shape, scale: scale * jax.random.normal(next(ks), shape, f32)
    gain = lambda shape: 1.0 + 0.02 * jax.random.normal(next(ks), shape, f32)
    D = D_MODEL
    return {
        'x_prompt': nrm((BATCH, SEQ, D), 1.0),
        'x_sample': nrm((DEC_BATCH, DEC_SEQ, D), 1.0),
        'cache_a_k': nrm((DEC_BATCH, N_EVEN, PAST_LEN, KV_HEADS, HEAD_DIM), 1.0),
        'cache_a_v': nrm((DEC_BATCH, N_EVEN, PAST_LEN, KV_HEADS, HEAD_DIM), 1.0),
        'cache_c_kv': nrm((DEC_BATCH, N_ODD, PAST_LEN, KV_LORA), 1.0),
        'cache_c_pe': nrm((DEC_BATCH, N_ODD, PAST_LEN, QK_ROPE), 1.0),
        'cache_d_k': nrm((DEC_BATCH, N_ODD, PAST_LEN, KV_HEADS, HEAD_DIM), 1.0),
        'cache_d_v': nrm((DEC_BATCH, N_ODD, PAST_LEN, KV_HEADS, HEAD_DIM), 1.0),
        'c': nrm((DEC_BATCH, D), 1.0),
        'c_ctx': nrm((D,), 1.0),
        'w_mod': nrm((DEPTH, D, N_MOD * D), 0.5 * D ** -0.5),
        'b_mod': nrm((DEPTH, N_MOD * D), 0.02),
        'norm1_g': gain((DEPTH, D)),
        'norm2_g': gain((DEPTH, D)),
        'w_in_even': nrm((N_EVEN, D, EVEN_IN), D ** -0.5),
        'w_out_even': nrm((N_EVEN, EVEN_OUT, D), EVEN_OUT ** -0.5),
        'a_q_norm': gain((N_EVEN, HEAD_DIM)),
        'a_k_norm': gain((N_EVEN, HEAD_DIM)),
        'b_conv': nrm((N_EVEN, CONV_W, B_WIDTH), 0.5),
        'w_in_odd': nrm((N_ODD, D, ODD_IN), D ** -0.5),
        'w_out_odd': nrm((N_ODD, ODD_OUT, D), ODD_OUT ** -0.5),
        'c_q_a_norm': gain((N_ODD, Q_LORA)),
        'c_kv_a_norm': gain((N_ODD, KV_LORA)),
        'c_w_q_b': nrm((N_ODD, Q_LORA, C_HEADS * QK_HEAD), Q_LORA ** -0.5),
        'c_w_kv_b': nrm((N_ODD, KV_LORA, C_HEADS * (QK_NOPE + V_DIM)), KV_LORA ** -0.5),
        'c_q_norm': gain((N_ODD, QK_HEAD)),
        'c_k_norm': gain((N_ODD, QK_HEAD)),
        'd_q_norm': gain((N_ODD, HEAD_DIM)),
        'd_k_norm': gain((N_ODD, HEAD_DIM)),
        'd_sink': nrm((N_ODD, D_HEADS), 0.5),
        'w_group_router': nrm((DEPTH, D, N_GROUPS), D ** -0.5),
        'b_group_router': nrm((DEPTH, N_GROUPS), 0.01),
        'w_expert_router': nrm((DEPTH, D, N_EXPERTS), D ** -0.5),
        'b_expert_router': nrm((DEPTH, N_EXPERTS), 0.01),
        'w_expert_gate': nrm((DEPTH, N_EXPERTS, D, D_EXPERT), D ** -0.5),
        'w_expert_up': nrm((DEPTH, N_EXPERTS, D, D_EXPERT), D ** -0.5),
        'w_expert_down': nrm((DEPTH, N_EXPERTS, D_EXPERT, D), D_EXPERT ** -0.5),
    }


def reference(x_prompt, x_sample, cache_a_k, cache_a_v, cache_c_kv, cache_c_pe, cache_d_k, cache_d_v,
              c, c_ctx, w_mod, b_mod, norm1_g, norm2_g, w_in_even, w_out_even, a_q_norm, a_k_norm, b_conv,
              w_in_odd, w_out_odd, c_q_a_norm, c_kv_a_norm, c_w_q_b, c_w_kv_b, c_q_norm, c_k_norm,
              d_q_norm, d_k_norm, d_sink, w_group_router, b_group_router, w_expert_router, b_expert_router,
              w_expert_gate, w_expert_up, w_expert_down):
    n_lat = x_sample.shape[1]
    rope_hd = axial_rope_tables(n_lat, HEAD_DIM)
    rope_pe = axial_rope_tables(n_lat, QK_ROPE)
    xp, xs = x_prompt, x_sample
    a_k, a_v, c_kv, c_pe, d_k, d_v = [], [], [], [], [], []
    for l in range(DEPTH):
        mp = modulation(c_ctx[None, :], w_mod[l], b_mod[l])
        ms = modulation(c, w_mod[l], b_mod[l])
        hp = adaln(xp, norm1_g[l], mp[0], mp[1])
        hs = adaln(xs, norm1_g[l], ms[0], ms[1])
        i = l // 2
        if l % 2 == 0:
            ew = (w_in_even[i], w_out_even[i], a_q_norm[i], a_k_norm[i], b_conv[i])
            op, (kp_, vp_) = even_mixer(hp, *ew)
            os_, _ = even_mixer(hs, *ew, ctx=(cache_a_k[:, i], cache_a_v[:, i]), rope=rope_hd)
            a_k.append(kp_)
            a_v.append(vp_)
        else:
            ow = (w_in_odd[i], w_out_odd[i], c_q_a_norm[i], c_kv_a_norm[i], c_w_q_b[i], c_w_kv_b[i],
                  c_q_norm[i], c_k_norm[i], d_q_norm[i], d_k_norm[i], d_sink[i])
            op, (ckv_, kpe_, dk_, dv_) = odd_mixer(hp, *ow)
            os_, _ = odd_mixer(hs, *ow, ctx=(cache_c_kv[:, i], cache_c_pe[:, i], cache_d_k[:, i], cache_d_v[:, i]),
                               rope_hd=rope_hd, rope_pe=rope_pe)
            c_kv.append(ckv_)
            c_pe.append(kpe_)
            d_k.append(dk_)
            d_v.append(dv_)
        xp = xp + mp[2] * op
        xs = xs + ms[2] * os_
        mw = (w_group_router[l], b_group_router[l], w_expert_router[l], b_expert_router[l],
              w_expert_gate[l], w_expert_up[l], w_expert_down[l])
        xp = xp + mp[5] * moe_ffn(adaln(xp, norm2_g[l], mp[3], mp[4]), *mw)
        xs = xs + ms[5] * moe_ffn(adaln(xs, norm2_g[l], ms[3], ms[4]), *mw)
    return (xp, xs, jnp.stack(a_k, axis=1), jnp.stack(a_v, axis=1), jnp.stack(c_kv, axis=1),
            jnp.stack(c_pe, axis=1), jnp.stack(d_k, axis=1), jnp.stack(d_v, axis=1))
```

```python
import functools

import jax
import jax.numpy as jnp
import numpy as np
from jax import lax
from jax.experimental import pallas as pl
from jax.experimental.pallas import tpu as pltpu

F32 = jnp.float32
BF16 = jnp.bfloat16

HEAD = 128
LANES = 128
EPS = 1e-6
ROPE_THETA = 10000.0
GRID_W = 64
NEG_INF = -1e30
N_MOD = 6
TOP_K = 2
N_GROUPS = 4
EXPERTS_PER_GROUP = 4
N_EXPERTS = N_GROUPS * EXPERTS_PER_GROUP
WINDOW = 128
VMEM_LIMIT = 56 * 1024 * 1024

TM = 1024
TM_OUT = 512
TN_OUT = 512
TM_CMB = 256
TB = 256
TN_EVEN = 768
TN_ODD = 1024
MOD_TN = 1024
MOD_ROWS = 16


def _cp(n_axes, vmem=VMEM_LIMIT):
    return pltpu.CompilerParams(dimension_semantics=("arbitrary",) * n_axes,
                                vmem_limit_bytes=vmem)


def _silu(x):
    return x * (1.0 / (1.0 + jnp.exp(-x)))


def _rms_scale(sumsq, n):
    return lax.rsqrt(sumsq * (1.0 / n) + EPS)


def _rope(y, cos, sin, q):
    lane = lax.broadcasted_iota(jnp.int32, y.shape, 1)
    first = jnp.bitwise_and(lane, 2 * q - 1) < q
    sw = jnp.where(first, pltpu.roll(y, LANES - q, 1), pltpu.roll(y, q, 1))
    return y * cos + sw * sin


def _rope_tables(n_tok, rot_dim):
    rows = n_tok // GRID_W
    row_id = jnp.repeat(jnp.arange(rows, dtype=F32), GRID_W)
    col_id = jnp.tile(jnp.arange(GRID_W, dtype=F32), rows)
    axis_dim = rot_dim // 2
    inv_freq = ROPE_THETA ** (-jnp.arange(0, axis_dim, 2, dtype=F32) / axis_dim)
    ang_r = row_id[:, None] * inv_freq[None, :]
    ang_c = col_id[:, None] * inv_freq[None, :]
    cr, sr, cc, sc = jnp.cos(ang_r), jnp.sin(ang_r), jnp.cos(ang_c), jnp.sin(ang_c)
    cos = jnp.concatenate([cr, cr, cc, cc], axis=-1)
    sin = jnp.concatenate([-sr, sr, -sc, sc], axis=-1)
    pad = LANES - rot_dim
    if pad:
        cos = jnp.pad(cos, ((0, 0), (0, pad)))
        sin = jnp.pad(sin, ((0, 0), (0, pad)))
    return cos, sin


def _mod_kernel(c_ref, w_ref, b_ref, o_ref):
    s = _silu(c_ref[...]).astype(BF16)
    o_ref[...] = jnp.dot(s, w_ref[...].astype(BF16), preferred_element_type=F32) + b_ref[...]


def _modulation(cvec, w_mod, b_mod):
    depth, d, n = w_mod.shape
    return pl.pallas_call(
        _mod_kernel,
        out_shape=jax.ShapeDtypeStruct((depth, MOD_ROWS, n), F32),
        grid=(depth, n // MOD_TN),
        in_specs=[pl.BlockSpec((MOD_ROWS, d), lambda l, j: (0, 0)),
                  pl.BlockSpec((None, d, MOD_TN), lambda l, j: (l, 0, j)),
                  pl.BlockSpec((None, 1, MOD_TN), lambda l, j: (l, 0, j))],
        out_specs=pl.BlockSpec((None, MOD_ROWS, MOD_TN), lambda l, j: (l, 0, j)),
        compiler_params=_cp(2),
        name="modulation",
    )(cvec, w_mod, b_mod.reshape(depth, 1, n))


def _adaln(x, g, shift, scale):
    ss = jnp.sum(x * x, axis=-1, keepdims=True)
    y = (x * _rms_scale(ss, x.shape[-1])) * g
    return y * (1.0 + scale) + shift


def _pre_kernel(x_ref, g_ref, sh_ref, sc_ref, h_ref):
    h_ref[...] = _adaln(x_ref[...], g_ref[...], sh_ref[0], sc_ref[0]).astype(BF16)


def _conv_gate(acc, wc_ref, seq):
    c = acc.shape[1] // 3
    u, gb, gc = acc[:, :c], acc[:, c:2 * c], acc[:, 2 * c:]
    z = gc * u
    rows = z.shape[0]
    pos = jnp.bitwise_and(lax.broadcasted_iota(jnp.int32, z.shape, 0), seq - 1)
    zp = jnp.where(pos == 0, 0.0, pltpu.roll(z, 1, 0))
    zn = jnp.where(pos == seq - 1, 0.0, pltpu.roll(z, rows - 1, 0))
    w = wc_ref[...]
    y = zp * w[0:1, :] + z * w[1:2, :] + zn * w[2:3, :]
    return gb * y


def _in_even_kernel(npb, seq_p, seq_s, n_qkv, n_kv_heads,
                    h_ref, w_ref, gcol_ref, cos_ref, sin_ref, wc_ref,
                    qkv_ref, kvf_ref, b_ref):
    m = pl.program_id(0)
    n = pl.program_id(1)
    acc = jnp.dot(h_ref[...], w_ref[...], preferred_element_type=F32)
    heads = acc.shape[1] // HEAD

    def qkv_epilogue(sample):
        ys = []
        for j in range(heads):
            sl = slice(j * HEAD, (j + 1) * HEAD)
            a = acc[:, sl]
            hidx = n * heads + j
            normed = jnp.logical_or(hidx < n_kv_heads, hidx >= 2 * n_kv_heads)
            ss = jnp.sum(a * a, axis=-1, keepdims=True)
            y = (a * _rms_scale(ss, HEAD)) * gcol_ref[:, sl]
            if sample:
                y = _rope(y, cos_ref[...], sin_ref[...], HEAD // 4)
            y = jnp.where(normed, y, a)
            qkv_ref[:, sl] = y.astype(BF16)
            ys.append(y)

        @pl.when(n < 2)
        def _():
            for j in range(heads):
                kvf_ref[:, j * HEAD:(j + 1) * HEAD] = ys[j]

    @pl.when(jnp.logical_and(n < n_qkv, m < npb))
    def _():
        qkv_epilogue(False)

    @pl.when(jnp.logical_and(n < n_qkv, m >= npb))
    def _():
        qkv_epilogue(True)

    @pl.when(jnp.logical_and(n >= n_qkv, m < npb))
    def _():
        b_ref[...] = _conv_gate(acc, wc_ref, seq_p).astype(BF16)

    @pl.when(jnp.logical_and(n >= n_qkv, m >= npb))
    def _():
        b_ref[...] = _conv_gate(acc, wc_ref, seq_s).astype(BF16)


def _in_odd_kernel(npb, kv_lora,
                   h_ref, w_ref, gcol_ref, cos_ref, sin_ref,
                   cqn_ref, o1b_ref, o1f_ref, qd_ref, o4b_ref, o4f_ref):
    m = pl.program_id(0)
    n = pl.program_id(1)
    acc = jnp.dot(h_ref[...], w_ref[...], preferred_element_type=F32)
    width = acc.shape[1]

    def head_norm(a, sl, sample):
        ss = jnp.sum(a * a, axis=-1, keepdims=True)
        y = (a * _rms_scale(ss, HEAD)) * gcol_ref[:, sl]
        if sample:
            y = _rope(y, cos_ref[...], sin_ref[...], HEAD // 4)
        return y

    @pl.when(n == 0)
    def _():
        ss = jnp.sum(acc * acc, axis=-1, keepdims=True)
        cqn_ref[...] = ((acc * _rms_scale(ss, width)) * gcol_ref[...]).astype(BF16)

    def block1(sample):
        a = acc[:, :kv_lora]
        ss = jnp.sum(a * a, axis=-1, keepdims=True)
        y = (a * _rms_scale(ss, kv_lora)) * gcol_ref[:, :kv_lora]
        o1f_ref[:, :kv_lora] = y
        o1b_ref[:, :kv_lora] = y.astype(BF16)
        for j in range(kv_lora // HEAD, width // HEAD):
            sl = slice(j * HEAD, (j + 1) * HEAD)
            y = head_norm(acc[:, sl], sl, sample)
            o1f_ref[:, sl] = y
            o1b_ref[:, sl] = y.astype(BF16)

    def blockq(sample):
        for j in range(width // HEAD):
            sl = slice(j * HEAD, (j + 1) * HEAD)
            qd_ref[:, sl] = head_norm(acc[:, sl], sl, sample).astype(BF16)

    @pl.when(jnp.logical_and(n == 1, m < npb))
    def _():
        block1(False)

    @pl.when(jnp.logical_and(n == 1, m >= npb))
    def _():
        block1(True)

    is_q = jnp.logical_or(n == 2, n == 3)

    @pl.when(jnp.logical_and(is_q, m < npb))
    def _():
        blockq(False)

    @pl.when(jnp.logical_and(is_q, m >= npb))
    def _():
        blockq(True)

    @pl.when(n == 4)
    def _():
        o4f_ref[...] = acc
        o4b_ref[...] = acc.astype(BF16)


def _qb_kernel(npb, n_heads, qk_head,
               x_ref, w_ref, g_ref, cos_ref, sin_ref, q_ref):
    m = pl.program_id(0)
    acc = jnp.dot(x_ref[...], w_ref[...], preferred_element_type=F32)

    def body(sample):
        for h in range(n_heads):
            s0 = slice(2 * h * HEAD, (2 * h + 1) * HEAD)
            s1 = slice((2 * h + 1) * HEAD, (2 * h + 2) * HEAD)
            nope, pe = acc[:, s0], acc[:, s1]
            ss = (jnp.sum(nope * nope, axis=-1, keepdims=True)
                  + jnp.sum(pe * pe, axis=-1, keepdims=True))
            r = _rms_scale(ss, qk_head)
            q_ref[:, s0] = ((nope * r) * g_ref[:, s0]).astype(BF16)
            y = (pe * r) * g_ref[:, s1]
            if sample:
                y = _rope(y, cos_ref[...], sin_ref[...], HEAD // 8)
            q_ref[:, s1] = y.astype(BF16)

    @pl.when(m < npb)
    def _():
        body(False)

    @pl.when(m >= npb)
    def _():
        body(True)


def _kvb_kernel(lo_s, hi_s, n_heads, qk_head,
                x_ref, pe_ref, w_ref, g_ref, cos_ref, sin_ref, k_ref, v_ref):
    m = pl.program_id(0)
    acc = jnp.dot(x_ref[...], w_ref[...], preferred_element_type=F32)
    kpe = pe_ref[...]
    ss_pe = jnp.sum(kpe * kpe, axis=-1, keepdims=True)
    nk = n_heads * HEAD

    def body(sample):
        pe_g = kpe * g_ref[:, HEAD:2 * HEAD]
        if sample:
            pe_g = _rope(pe_g, cos_ref[...], sin_ref[...], HEAD // 8)
        for h in range(n_heads):
            kn = acc[:, h * HEAD:(h + 1) * HEAD]
            ss = jnp.sum(kn * kn, axis=-1, keepdims=True) + ss_pe
            r = _rms_scale(ss, qk_head)
            k_ref[:, 2 * h * HEAD:(2 * h + 1) * HEAD] = ((kn * r) * g_ref[:, :HEAD]).astype(BF16)
            k_ref[:, (2 * h + 1) * HEAD:(2 * h + 2) * HEAD] = (pe_g * r).astype(BF16)
            v_ref[:, h * HEAD:(h + 1) * HEAD] = acc[:, nk + h * HEAD:nk + (h + 1) * HEAD].astype(BF16)

    is_s = jnp.logical_and(m >= lo_s, m < hi_s)

    @pl.when(jnp.logical_not(is_s))
    def _():
        body(False)

    @pl.when(is_s)
    def _():
        body(True)


def _softmax_pv(s, v, sink=None):
    mx = jnp.max(s, axis=-1, keepdims=True)
    if sink is not None:
        mx = jnp.maximum(mx, sink)
    p = jnp.exp(s - mx)
    l = jnp.sum(p, axis=-1, keepdims=True)
    if sink is not None:
        l = l + jnp.exp(sink - mx)
    p = p * (1.0 / l)
    return jnp.dot(p.astype(BF16), v, preferred_element_type=F32)


def _qk(q, k):
    return lax.dot_general(q, k, (((1,), (1,)), ((), ())), preferred_element_type=F32)


def _attn_prompt_kernel(n_seq, seq, groups, dk, scale, use_sink, *refs):
    if use_sink:
        sink_ref, q_ref, k_ref, v_ref, o_ref = refs
    else:
        q_ref, k_ref, v_ref, o_ref = refs
    j = pl.program_id(1)
    dv = v_ref.shape[1]
    for b in range(n_seq):
        rows = slice(b * seq, (b + 1) * seq)
        k = k_ref[rows, :]
        v = v_ref[rows, :]
        for g in range(groups):
            q = q_ref[rows, g * dk:(g + 1) * dk]
            s = _qk(q, k) * scale
            sink = sink_ref[j * groups + g] if use_sink else None
            o_ref[rows, g * dv:(g + 1) * dv] = _softmax_pv(s, v, sink).astype(o_ref.dtype)


def _attn_sample_kernel(seq, qblk, groups, dk, scale, *refs):
    q_ref, k_ref, v_ref, ck_ref, cv_ref, a_ref, o_ref = refs
    del a_ref
    dv = v_ref.shape[1]
    k = jnp.concatenate([k_ref[...], ck_ref[...].astype(BF16)], axis=0)
    v = jnp.concatenate([v_ref[...], cv_ref[...].astype(BF16)], axis=0)

    def step(i, carry):
        r0 = pl.multiple_of(i * qblk, qblk)
        for g in range(groups):
            q = q_ref[pl.ds(r0, qblk), g * dk:(g + 1) * dk]
            s = _qk(q, k) * scale
            o_ref[pl.ds(r0, qblk), g * dv:(g + 1) * dv] = _softmax_pv(s, v).astype(o_ref.dtype)
        return carry

    lax.fori_loop(0, seq // qblk, step, 0)


def _attn_window_kernel(seq, groups, scale, sink_ref, q_ref, k_ref, v_ref, ck_ref, cv_ref,
                        a_ref, o_ref):
    del a_ref
    j = pl.program_id(1)
    ck = ck_ref[...].astype(BF16)
    cv = cv_ref[...].astype(BF16)
    w = WINDOW
    rid = lax.shift_right_logical(lax.broadcasted_iota(jnp.int32, (groups * w, 1), 0),
                                  int(np.log2(w)))
    sink = jnp.zeros((groups * w, 1), F32)
    for g in range(groups):
        sink = jnp.where(rid == g, sink_ref[j * groups + g], sink)
    for qb in range(seq // w):
        lo = max(0, (qb - 1) * w)
        hi = min(seq, (qb + 2) * w)
        rows = slice(qb * w, (qb + 1) * w)
        q = jnp.concatenate([q_ref[rows, g * HEAD:(g + 1) * HEAD] for g in range(groups)], axis=0)
        kl = k_ref[lo:hi, :]
        vl = v_ref[lo:hi, :]
        s_loc = _qk(q, kl) * scale
        qpos = qb * w + jnp.bitwise_and(lax.broadcasted_iota(jnp.int32, s_loc.shape, 0), w - 1)
        kpos = lo + lax.broadcasted_iota(jnp.int32, s_loc.shape, 1)
        s_loc = jnp.where(jnp.abs(qpos - kpos) <= w, s_loc, NEG_INF)
        s_ctx = _qk(q, ck) * scale
        mx = jnp.maximum(jnp.maximum(jnp.max(s_loc, axis=-1, keepdims=True),
                                     jnp.max(s_ctx, axis=-1, keepdims=True)), sink)
        p_loc = jnp.exp(s_loc - mx)
        p_ctx = jnp.exp(s_ctx - mx)
        l = (jnp.sum(p_loc, axis=-1, keepdims=True) + jnp.sum(p_ctx, axis=-1, keepdims=True)
             + jnp.exp(sink - mx))
        inv = 1.0 / l
        o = (jnp.dot((p_loc * inv).astype(BF16), vl, preferred_element_type=F32)
             + jnp.dot((p_ctx * inv).astype(BF16), cv, preferred_element_type=F32))
        for g in range(groups):
            o_ref[rows, g * HEAD:(g + 1) * HEAD] = o[g * w:(g + 1) * w, :].astype(o_ref.dtype)


def _out_kernel(n_chunks, a_ref, b_ref, w_ref, x_ref, gate_ref, g2_ref, sh_ref, sc_ref,
                whi_ref, wlo_ref, rb_ref, xo_ref, h2_ref, lg_ref, xs_ref):
    n = pl.program_id(1)
    ka = a_ref.shape[1]
    acc = (jnp.dot(a_ref[...], w_ref[:ka, :], preferred_element_type=F32)
           + jnp.dot(b_ref[...], w_ref[ka:, :], preferred_element_type=F32))
    xn = x_ref[...] + gate_ref[0] * acc
    xo_ref[...] = xn
    xs_ref[n] = xn

    @pl.when(n == n_chunks - 1)
    def _():
        tn = xn.shape[1]
        ss = jnp.zeros((xn.shape[0], 1), F32)
        for c in range(n_chunks):
            xc = xs_ref[c]
            ss = ss + jnp.sum(xc * xc, axis=-1, keepdims=True)
        r = _rms_scale(ss, n_chunks * tn)
        lg = jnp.zeros(lg_ref.shape, F32) + rb_ref[...]
        for c in range(n_chunks):
            sl = slice(c * tn, (c + 1) * tn)
            y = (xs_ref[c] * r) * g2_ref[:, sl]
            h = y * (1.0 + sc_ref[0][:, sl]) + sh_ref[0][:, sl]
            hi = h.astype(BF16)
            lo = (h - hi.astype(F32)).astype(BF16)
            h2_ref[:, sl] = hi
            lg = lg + (jnp.dot(hi, whi_ref[sl, :], preferred_element_type=F32)
                       + jnp.dot(hi, wlo_ref[sl, :], preferred_element_type=F32)
                       + jnp.dot(lo, whi_ref[sl, :], preferred_element_type=F32))
        lg_ref[...] = lg


def _up_kernel(be_ref, nb_ref, x_ref, wg_ref, wu_ref, a_ref, wg_s, wu_s):
    b = pl.program_id(1)
    prev = be_ref[jnp.maximum(b - 1, 0)]
    fresh = jnp.logical_or(b == 0, be_ref[b] != prev)

    @pl.when(fresh)
    def _():
        wg_s[...] = wg_ref[...].astype(BF16)
        wu_s[...] = wu_ref[...].astype(BF16)

    @pl.when(b < nb_ref[0])
    def _():
        x = x_ref[...]
        g = jnp.dot(x, wg_s[...], preferred_element_type=F32)
        u = jnp.dot(x, wu_s[...], preferred_element_type=F32)
        a_ref[...] = (_silu(g) * u).astype(BF16)


def _down_kernel(be_ref, nb_ref, a_ref, wd_ref, y_ref, wd_s):
    b = pl.program_id(1)
    prev = be_ref[jnp.maximum(b - 1, 0)]
    fresh = jnp.logical_or(b == 0, be_ref[b] != prev)

    @pl.when(fresh)
    def _():
        wd_s[...] = wd_ref[...].astype(BF16)

    @pl.when(b < nb_ref[0])
    def _():
        y_ref[...] = jnp.dot(a_ref[...], wd_s[...], preferred_element_type=F32)


def _combine_kernel(with_next, x_ref, y0_ref, y1_ref, gw_ref, gate_ref, *refs):
    if with_next:
        g_ref, sh_ref, sc_ref, xo_ref, h_ref = refs
    else:
        (xo_ref,) = refs
    gw = gw_ref[...]
    moe = gw[:, 0:1] * y0_ref[...] + gw[:, 1:2] * y1_ref[...]
    xn = x_ref[...] + gate_ref[0] * moe
    xo_ref[...] = xn
    if with_next:
        h_ref[...] = _adaln(xn, g_ref[...], sh_ref[0], sc_ref[0]).astype(BF16)


def kernel(x_prompt, x_sample, cache_a_k, cache_a_v, cache_c_kv, cache_c_pe, cache_d_k, cache_d_v,
           c, c_ctx, w_mod, b_mod, norm1_g, norm2_g, w_in_even, w_out_even, a_q_norm, a_k_norm,
           b_conv, w_in_odd, w_out_odd, c_q_a_norm, c_kv_a_norm, c_w_q_b, c_w_kv_b, c_q_norm,
           c_k_norm, d_q_norm, d_k_norm, d_sink, w_group_router, b_group_router, w_expert_router,
           b_expert_router, w_expert_gate, w_expert_up, w_expert_down):
    batch, seq, d = x_prompt.shape
    dec_batch, dec_seq, _ = x_sample.shape
    past = cache_a_k.shape[2]
    depth = w_mod.shape[0]
    kvh = cache_a_k.shape[3]
    b_width = d // 2
    a_width = w_out_even.shape[1] - b_width
    a_heads = a_width // HEAD
    groups = a_heads // kvh
    kv_width = kvh * HEAD
    q_lora = c_q_a_norm.shape[1]
    kv_lora = c_kv_a_norm.shape[1]
    qk_head = c_q_norm.shape[1]
    qk_rope = cache_c_pe.shape[3]
    qk_nope = qk_head - qk_rope
    c_heads = c_w_q_b.shape[2] // qk_head
    v_dim = c_w_kv_b.shape[2] // c_heads - qk_nope
    d_width = d_sink.shape[1] * HEAD
    d_expert = w_expert_gate.shape[3]
    tp = batch * seq
    ts = dec_batch * dec_seq
    t = tp + ts
    assert qk_nope == HEAD and v_dim == HEAD and 2 * qk_rope == HEAD and d_width == a_width
    assert seq == past and dec_seq == TM and tp % TM == 0 and TM % seq == 0

    def row_of(tm):
        npb, bps = tp // tm, dec_seq // tm
        return lambda m: jnp.where(m < npb, 0, 1 + (m - npb) // bps)

    cvec = jnp.zeros((MOD_ROWS, d), F32).at[0].set(c_ctx).at[1:1 + dec_batch].set(c)
    mods = _modulation(cvec, w_mod, b_mod)
    n_rows = 1 + dec_batch
    modtab = (mods[:, :n_rows].reshape(depth, n_rows, N_MOD, d).transpose(0, 2, 1, 3)
              .reshape(depth, N_MOD * n_rows, 1, d))

    def mod_spec(j, tm, cols=None, col_axis=None):
        r = row_of(tm)
        if cols is None:
            return pl.BlockSpec((1, 1, d), lambda *g: (j * n_rows + r(g[0]), 0, 0))
        return pl.BlockSpec((1, 1, cols), lambda *g: (j * n_rows + r(g[0]), 0, g[col_axis]))

    cos_hd, sin_hd = _rope_tables(dec_seq, HEAD)
    cos_pe, sin_pe = _rope_tables(dec_seq, qk_rope)

    x = jnp.concatenate([x_prompt.reshape(tp, d), x_sample.reshape(ts, d)], axis=0)
    npb = tp // TM

    full = lambda shape: pl.BlockSpec(shape, lambda *g: (0,) * len(shape))

    h = pl.pallas_call(
        _pre_kernel,
        out_shape=jax.ShapeDtypeStruct((t, d), BF16),
        grid=(t // TM_OUT,),
        in_specs=[pl.BlockSpec((TM_OUT, d), lambda m: (m, 0)), full((1, d)),
                  mod_spec(0, TM_OUT), mod_spec(1, TM_OUT)],
        out_specs=pl.BlockSpec((TM_OUT, d), lambda m: (m, 0)),
        compiler_params=_cp(1),
        name="adaln_first",
    )(x, norm1_g[0:1], modtab[0], modtab[0])

    a_k, a_v, c_kv, c_pe, d_k, d_v = [], [], [], [], [], []

    for l in range(depth):
        i = l // 2
        mt = modtab[l]
        if l % 2 == 0:
            w = w_in_even[i]
            q_w, k_w, v_w = w[:, :a_width], w[:, a_width:a_width + kv_width], \
                w[:, a_width + kv_width:a_width + 2 * kv_width]
            base = a_width + 2 * kv_width
            cch = TN_EVEN // 3
            nch = b_width // cch
            ugg = w[:, base:].reshape(d, 3, nch, cch).transpose(0, 2, 1, 3).reshape(d, 3 * b_width)
            w_in = jnp.concatenate([k_w, v_w, q_w, ugg], axis=1).astype(BF16)
            n_qkv = (a_width + 2 * kv_width) // TN_EVEN
            gcol = jnp.concatenate([jnp.tile(a_k_norm[i], kvh), jnp.ones((kv_width,), F32),
                                    jnp.tile(a_q_norm[i], a_heads)])[None, :]
            n_in = w_in.shape[1] // TN_EVEN
            qkv, kvf, bconv = pl.pallas_call(
                functools.partial(_in_even_kernel, npb, seq, dec_seq, n_qkv, kvh),
                out_shape=(jax.ShapeDtypeStruct((t, a_width + 2 * kv_width), BF16),
                           jax.ShapeDtypeStruct((t, 2 * TN_EVEN), F32),
                           jax.ShapeDtypeStruct((t, b_width), BF16)),
                grid=(t // TM, n_in),
                in_specs=[pl.BlockSpec((TM, d), lambda m, n: (m, 0)),
                          pl.BlockSpec((d, TN_EVEN), lambda m, n: (0, n)),
                          pl.BlockSpec((1, TN_EVEN), lambda m, n: (0, jnp.minimum(n, n_qkv - 1))),
                          full((TM, LANES)), full((TM, LANES)),
                          pl.BlockSpec((3, cch), lambda m, n: (0, jnp.maximum(n - n_qkv, 0)))],
                out_specs=(pl.BlockSpec((TM, TN_EVEN), lambda m, n: (m, jnp.minimum(n, n_qkv - 1))),
                           pl.BlockSpec((TM, TN_EVEN), lambda m, n: (m, jnp.minimum(n, 1))),
                           pl.BlockSpec((TM, cch), lambda m, n: (m, jnp.maximum(n - n_qkv, 0)))),
                compiler_params=_cp(2),
                name="in_proj_even",
            )(h, w_in, gcol, cos_hd, sin_hd, b_conv[i])
            a_k.append(kvf[:tp, :kv_width].reshape(batch, seq, kvh, HEAD))
            a_v.append(kvf[:tp, kv_width:2 * kv_width].reshape(batch, seq, kvh, HEAD))

            gw_ = groups * HEAD
            kb, vb, qb0 = 0, kv_width // HEAD, 2 * kv_width // gw_
            scale = HEAD ** -0.5
            att = pl.pallas_call(
                functools.partial(_attn_prompt_kernel, TM // seq, seq, groups, HEAD, scale, False),
                out_shape=jax.ShapeDtypeStruct((t, a_width), BF16),
                grid=(npb, kvh),
                in_specs=[pl.BlockSpec((TM, gw_), lambda m, j: (m, qb0 + j)),
                          pl.BlockSpec((TM, HEAD), lambda m, j: (m, kb + j)),
                          pl.BlockSpec((TM, HEAD), lambda m, j: (m, vb + j))],
                out_specs=pl.BlockSpec((TM, gw_), lambda m, j: (m, j)),
                compiler_params=_cp(2),
                name="attn_a_prompt",
            )(qkv, qkv, qkv)
            ck = cache_a_k[:, i].reshape(dec_batch, past, kv_width)
            cv = cache_a_v[:, i].reshape(dec_batch, past, kv_width)
            att = pl.pallas_call(
                functools.partial(_attn_sample_kernel, dec_seq, 256, groups, HEAD, scale),
                out_shape=jax.ShapeDtypeStruct((t, a_width), BF16),
                grid=(dec_batch, kvh),
                in_specs=[pl.BlockSpec((TM, gw_), lambda b, j: (npb + b, qb0 + j)),
                          pl.BlockSpec((TM, HEAD), lambda b, j: (npb + b, kb + j)),
                          pl.BlockSpec((TM, HEAD), lambda b, j: (npb + b, vb + j)),
                          pl.BlockSpec((None, past, HEAD), lambda b, j: (b, 0, j)),
                          pl.BlockSpec((None, past, HEAD), lambda b, j: (b, 0, j)),
                          pl.BlockSpec(memory_space=pl.ANY)],
                out_specs=pl.BlockSpec((TM, gw_), lambda b, j: (npb + b, j)),
                input_output_aliases={5: 0},
                compiler_params=_cp(2),
                name="attn_a_sample",
            )(qkv, qkv, qkv, ck, cv, att)
            lhs_a, lhs_b = att, bconv
            w_out = w_out_even[i].astype(BF16)
        else:
            w = w_in_odd[i]
            o = np.cumsum([0, q_lora, kv_lora, qk_rope, d_width, kv_width, kv_width])
            cq_w, ckv_w, kpe_w, dq_w, dk_w, dv_w = [w[:, o[s]:o[s + 1]] for s in range(6)]
            tail = TN_ODD - kv_width - qk_rope
            w_in = jnp.concatenate([cq_w, ckv_w, dk_w, dq_w, dv_w, kpe_w,
                                    jnp.zeros((d, tail), F32)], axis=1).astype(BF16)
            assert q_lora == TN_ODD and kv_lora + kv_width == TN_ODD and d_width == 2 * TN_ODD
            gcol = jnp.concatenate([c_q_a_norm[i], c_kv_a_norm[i], jnp.tile(d_k_norm[i], kvh),
                                    jnp.tile(d_q_norm[i], d_width // HEAD),
                                    jnp.ones((TN_ODD,), F32)])[None, :]
            n_in = w_in.shape[1] // TN_ODD
            blk = lambda fn: pl.BlockSpec((TM_OUT, TN_ODD), fn)
            spb = dec_seq // TM_OUT
            cqn, o1b, o1f, qd, o4b, o4f = pl.pallas_call(
                functools.partial(_in_odd_kernel, tp // TM_OUT, kv_lora),
                out_shape=(jax.ShapeDtypeStruct((t, TN_ODD), BF16),
                           jax.ShapeDtypeStruct((t, TN_ODD), BF16),
                           jax.ShapeDtypeStruct((t, TN_ODD), F32),
                           jax.ShapeDtypeStruct((t, d_width), BF16),
                           jax.ShapeDtypeStruct((t, TN_ODD), BF16),
                           jax.ShapeDtypeStruct((t, TN_ODD), F32)),
                grid=(t // TM_OUT, n_in),
                in_specs=[pl.BlockSpec((TM_OUT, d), lambda m, n: (m, 0)),
                          pl.BlockSpec((d, TN_ODD), lambda m, n: (0, n)),
                          pl.BlockSpec((1, TN_ODD), lambda m, n: (0, n)),
                          pl.BlockSpec((TM_OUT, LANES), lambda m, n: (m % spb, 0)),
                          pl.BlockSpec((TM_OUT, LANES), lambda m, n: (m % spb, 0))],
                out_specs=(blk(lambda m, n: (m, 0)), blk(lambda m, n: (m, 0)),
                           blk(lambda m, n: (m, 0)),
                           blk(lambda m, n: (m, jnp.clip(n - 2, 0, 1))),
                           blk(lambda m, n: (m, 0)), blk(lambda m, n: (m, 0))),
                compiler_params=_cp(2),
                name="in_proj_odd",
            )(h, w_in, gcol, cos_hd, sin_hd)
            c_kv.append(o1f[:tp, :kv_lora].reshape(batch, seq, kv_lora))
            d_k.append(o1f[:tp, kv_lora:].reshape(batch, seq, kvh, HEAD))
            d_v.append(o4f[:tp, :kv_width].reshape(batch, seq, kvh, HEAD))
            c_pe.append(o4f[:tp, kv_width:kv_width + qk_rope].reshape(batch, seq, qk_rope))

            wq = c_w_q_b[i].reshape(q_lora, c_heads, qk_head)
            wq = jnp.pad(wq, ((0, 0), (0, 0), (0, 2 * HEAD - qk_head)))
            wq = wq.reshape(q_lora, c_heads * 2 * HEAD).astype(BF16)
            gq = jnp.tile(jnp.pad(c_q_norm[i], (0, 2 * HEAD - qk_head)), c_heads)[None, :]
            cw = c_heads * 2 * HEAD
            q_c = pl.pallas_call(
                functools.partial(_qb_kernel, tp // TM_OUT, c_heads, qk_head),
                out_shape=jax.ShapeDtypeStruct((t, cw), BF16),
                grid=(t // TM_OUT,),
                in_specs=[pl.BlockSpec((TM_OUT, q_lora), lambda m: (m, 0)),
                          full((q_lora, cw)), full((1, cw)),
                          pl.BlockSpec((TM_OUT, LANES), lambda m: (m % (dec_seq // TM_OUT), 0)),
                          pl.BlockSpec((TM_OUT, LANES), lambda m: (m % (dec_seq // TM_OUT), 0))],
                out_specs=pl.BlockSpec((TM_OUT, cw), lambda m: (m, 0)),
                compiler_params=_cp(1),
                name="mla_q_up",
            )(cqn, wq, gq, cos_pe, sin_pe)

            wkv = c_w_kv_b[i].reshape(kv_lora, c_heads, qk_nope + v_dim)
            wkv = jnp.concatenate([wkv[:, :, :qk_nope].reshape(kv_lora, c_heads * qk_nope),
                                   wkv[:, :, qk_nope:].reshape(kv_lora, c_heads * v_dim)],
                                  axis=1).astype(BF16)
            n_ctx = dec_batch * past
            ckv_all = jnp.concatenate([o1b[:, :kv_lora],
                                       cache_c_kv[:, i].reshape(n_ctx, kv_lora).astype(BF16)], axis=0)
            kpe_all = jnp.concatenate(
                [o4f[:, kv_width:kv_width + HEAD],
                 jnp.pad(cache_c_pe[:, i].reshape(n_ctx, qk_rope), ((0, 0), (0, HEAD - qk_rope)))],
                axis=0)
            gk = jnp.pad(c_k_norm[i], (0, 2 * HEAD - qk_head))[None, :]
            t2 = t + n_ctx
            spb = dec_seq // TM_OUT
            k_c, v_c = pl.pallas_call(
                functools.partial(_kvb_kernel, tp // TM_OUT, t // TM_OUT, c_heads, qk_head),
                out_shape=(jax.ShapeDtypeStruct((t2, cw), BF16),
                           jax.ShapeDtypeStruct((t2, c_heads * v_dim), BF16)),
                grid=(t2 // TM_OUT,),
                in_specs=[pl.BlockSpec((TM_OUT, kv_lora), lambda m: (m, 0)),
                          pl.BlockSpec((TM_OUT, HEAD), lambda m: (m, 0)),
                          full((kv_lora, c_heads * (qk_nope + v_dim))), full((1, 2 * HEAD)),
                          pl.BlockSpec((TM_OUT, LANES), lambda m: (m % spb, 0)),
                          pl.BlockSpec((TM_OUT, LANES), lambda m: (m % spb, 0))],
                out_specs=(pl.BlockSpec((TM_OUT, cw), lambda m: (m, 0)),
                           pl.BlockSpec((TM_OUT, c_heads * v_dim), lambda m: (m, 0))),
                compiler_params=_cp(1),
                name="mla_kv_up",
            )(ckv_all, kpe_all, wkv, gk, cos_pe, sin_pe)

            scale_c = qk_head ** -0.5
            oc = pl.pallas_call(
                functools.partial(_attn_prompt_kernel, TM // seq, seq, 1, 2 * HEAD, scale_c, False),
                out_shape=jax.ShapeDtypeStruct((t, c_heads * v_dim), BF16),
                grid=(npb, c_heads),
                in_specs=[pl.BlockSpec((TM, 2 * HEAD), lambda m, j: (m, j)),
                          pl.BlockSpec((TM, 2 * HEAD), lambda m, j: (m, j)),
                          pl.BlockSpec((TM, v_dim), lambda m, j: (m, j))],
                out_specs=pl.BlockSpec((TM, v_dim), lambda m, j: (m, j)),
                compiler_params=_cp(2),
                name="attn_c_prompt",
            )(q_c, k_c, v_c)
            cb0 = t // past
            oc = pl.pallas_call(
                functools.partial(_attn_sample_kernel, dec_seq, 256, 1, 2 * HEAD, scale_c),
                out_shape=jax.ShapeDtypeStruct((t, c_heads * v_dim), BF16),
                grid=(dec_batch, c_heads),
                in_specs=[pl.BlockSpec((TM, 2 * HEAD), lambda b, j: (npb + b, j)),
                          pl.BlockSpec((TM, 2 * HEAD), lambda b, j: (npb + b, j)),
                          pl.BlockSpec((TM, v_dim), lambda b, j: (npb + b, j)),
                          pl.BlockSpec((past, 2 * HEAD), lambda b, j: (cb0 + b, j)),
                          pl.BlockSpec((past, v_dim), lambda b, j: (cb0 + b, j)),
                          pl.BlockSpec(memory_space=pl.ANY)],
                out_specs=pl.BlockSpec((TM, v_dim), lambda b, j: (npb + b, j)),
                input_output_aliases={5: 0},
                compiler_params=_cp(2),
                name="attn_c_sample",
            )(q_c, k_c, v_c, k_c, v_c, oc)

            gw_ = groups * HEAD
            scale = HEAD ** -0.5
            sink = d_sink[i]
            kdb = kv_lora // HEAD
            od = pl.pallas_call(
                functools.partial(_attn_prompt_kernel, TM // seq, seq, groups, HEAD, scale, True),
                out_shape=jax.ShapeDtypeStruct((t, d_width), BF16),
                grid_spec=pltpu.PrefetchScalarGridSpec(
                    num_scalar_prefetch=1, grid=(npb, kvh),
                    in_specs=[pl.BlockSpec((TM, gw_), lambda m, j, s: (m, j)),
                              pl.BlockSpec((TM, HEAD), lambda m, j, s: (m, kdb + j)),
                              pl.BlockSpec((TM, HEAD), lambda m, j, s: (m, j))],
                    out_specs=pl.BlockSpec((TM, gw_), lambda m, j, s: (m, j))),
                compiler_params=_cp(2),
                name="attn_d_prompt",
            )(sink, qd, o1b, o4b)
            ck = cache_d_k[:, i].reshape(dec_batch, past, kv_width)
            cv = cache_d_v[:, i].reshape(dec_batch, past, kv_width)
            od = pl.pallas_call(
                functools.partial(_attn_window_kernel, dec_seq, groups, scale),
                out_shape=jax.ShapeDtypeStruct((t, d_width), BF16),
                grid_spec=pltpu.PrefetchScalarGridSpec(
                    num_scalar_prefetch=1, grid=(dec_batch, kvh),
                    in_specs=[pl.BlockSpec((TM, gw_), lambda b, j, s: (npb + b, j)),
                              pl.BlockSpec((TM, HEAD), lambda b, j, s: (npb + b, kdb + j)),
                              pl.BlockSpec((TM, HEAD), lambda b, j, s: (npb + b, j)),
                              pl.BlockSpec((None, past, HEAD), lambda b, j, s: (b, 0, j)),
                              pl.BlockSpec((None, past, HEAD), lambda b, j, s: (b, 0, j)),
                              pl.BlockSpec(memory_space=pl.ANY)],
                    out_specs=pl.BlockSpec((TM, gw_), lambda b, j, s: (npb + b, j))),
                input_output_aliases={6: 0},
                compiler_params=_cp(2),
                name="attn_d_sample",
            )(sink, qd, o1b, o4b, ck, cv, od)
            lhs_a, lhs_b = oc, od
            w_out = w_out_odd[i].astype(BF16)

        wr = jnp.concatenate([w_group_router[l], w_expert_router[l]], axis=1)
        n_r = wr.shape[1]
        wr = jnp.pad(wr, ((0, 0), (0, LANES - n_r)))
        wr_hi = wr.astype(BF16)
        wr_lo = (wr - wr_hi.astype(F32)).astype(BF16)
        rb = jnp.pad(jnp.concatenate([b_group_router[l], b_expert_router[l]]), (0, LANES - n_r))[None]
        n_out = d // TN_OUT
        ka = lhs_a.shape[1]
        x, h2, logits = pl.pallas_call(
            functools.partial(_out_kernel, n_out),
            out_shape=(jax.ShapeDtypeStruct((t, d), F32), jax.ShapeDtypeStruct((t, d), BF16),
                       jax.ShapeDtypeStruct((t, LANES), F32)),
            grid=(t // TM_OUT, n_out),
            in_specs=[pl.BlockSpec((TM_OUT, ka), lambda m, n: (m, 0)),
                      pl.BlockSpec((TM_OUT, lhs_b.shape[1]), lambda m, n: (m, 0)),
                      pl.BlockSpec((w_out.shape[0], TN_OUT), lambda m, n: (0, n)),
                      pl.BlockSpec((TM_OUT, TN_OUT), lambda m, n: (m, n)),
                      mod_spec(2, TM_OUT, TN_OUT, 1),
                      full((1, d)), mod_spec(3, TM_OUT), mod_spec(4, TM_OUT),
                      full((d, LANES)), full((d, LANES)), full((1, LANES))],
            out_specs=(pl.BlockSpec((TM_OUT, TN_OUT), lambda m, n: (m, n)),
                       pl.BlockSpec((TM_OUT, d), lambda m, n: (m, 0)),
                       pl.BlockSpec((TM_OUT, LANES), lambda m, n: (m, 0))),
            scratch_shapes=[pltpu.VMEM((n_out, TM_OUT, TN_OUT), F32)],
            compiler_params=_cp(2),
            name="out_proj",
        )(lhs_a, lhs_b, w_out, x, mt, norm2_g[l:l + 1], mt, mt, wr_hi, wr_lo, rb)

        g_logits = logits[:, :N_GROUPS]
        e_logits = logits[:, N_GROUPS:N_GROUPS + N_EXPERTS].reshape(t, N_GROUPS, EXPERTS_PER_GROUP)
        g_prob = jax.nn.softmax(g_logits, axis=-1)
        g_idx = jnp.argmax(g_logits, axis=-1)
        e_in = jnp.take_along_axis(e_logits, g_idx[:, None, None], axis=1)[:, 0]
        top_p, top_i = lax.top_k(jax.nn.softmax(e_in, axis=-1), TOP_K)
        gate = (jnp.take_along_axis(g_prob, g_idx[:, None], axis=1) * top_p
                / jnp.sum(top_p, axis=-1, keepdims=True))
        eid = (g_idx[:, None] * EXPERTS_PER_GROUP + top_i).reshape(-1).astype(jnp.int32)
        n_pairs = t * TOP_K
        onehot = (eid[:, None] == jnp.arange(N_EXPERTS, dtype=jnp.int32)[None, :]).astype(jnp.int32)
        rank = jnp.take_along_axis(jnp.cumsum(onehot, axis=0) - onehot, eid[:, None], axis=1)[:, 0]
        counts = jnp.sum(onehot, axis=0)
        nblk_e = (counts + TB - 1) // TB
        blk_end = jnp.cumsum(nblk_e)
        blk_start = blk_end - nblk_e
        dest = (blk_start[eid] * TB + rank).astype(jnp.int32)
        n_blocks = (n_pairs + N_EXPERTS * (TB - 1) + TB - 1) // TB
        n_slots = n_blocks * TB
        slot_token = jnp.zeros((n_slots,), jnp.int32).at[dest].set(
            jnp.arange(n_pairs, dtype=jnp.int32) // TOP_K)
        nb_used = blk_end[-1:].astype(jnp.int32)
        block_e = jnp.minimum(jnp.searchsorted(blk_end, jnp.arange(n_blocks), side='right'),
                              N_EXPERTS - 1).astype(jnp.int32)
        block_e = jnp.where(jnp.arange(n_blocks) < nb_used[0], block_e,
                            block_e[jnp.maximum(nb_used[0] - 1, 0)])
        xs = h2[slot_token]

        fc = d_expert // 2
        used = lambda b, nb: jnp.minimum(b, nb[0] - 1)
        act = pl.pallas_call(
            _up_kernel,
            out_shape=jax.ShapeDtypeStruct((n_slots, d_expert), BF16),
            grid_spec=pltpu.PrefetchScalarGridSpec(
                num_scalar_prefetch=2, grid=(d_expert // fc, n_blocks),
                in_specs=[pl.BlockSpec((TB, d), lambda cc, b, be, nb: (used(b, nb), 0)),
                          pl.BlockSpec((None, None, d, fc), lambda cc, b, be, nb: (l, be[b], 0, cc)),
                          pl.BlockSpec((None, None, d, fc), lambda cc, b, be, nb: (l, be[b], 0, cc))],
                out_specs=pl.BlockSpec((TB, fc), lambda cc, b, be, nb: (used(b, nb), cc)),
                scratch_shapes=[pltpu.VMEM((d, fc), BF16), pltpu.VMEM((d, fc), BF16)]),
            compiler_params=_cp(2),
            name="expert_up",
        )(block_e, nb_used, xs, w_expert_gate, w_expert_up)
        dn = d // 2
        yb = pl.pallas_call(
            _down_kernel,
            out_shape=jax.ShapeDtypeStruct((n_slots, d), F32),
            grid_spec=pltpu.PrefetchScalarGridSpec(
                num_scalar_prefetch=2, grid=(d // dn, n_blocks),
                in_specs=[pl.BlockSpec((TB, d_expert), lambda cc, b, be, nb: (used(b, nb), 0)),
                          pl.BlockSpec((None, None, d_expert, dn),
                                       lambda cc, b, be, nb: (l, be[b], 0, cc))],
                out_specs=pl.BlockSpec((TB, dn), lambda cc, b, be, nb: (used(b, nb), cc)),
                scratch_shapes=[pltpu.VMEM((d_expert, dn), BF16)]),
            compiler_params=_cp(2),
            name="expert_down",
        )(block_e, nb_used, act, w_expert_down)

        ypair = yb[dest].reshape(t, TOP_K * d)
        gw = jnp.pad(gate.astype(F32), ((0, 0), (0, LANES - TOP_K)))
        with_next = l + 1 < depth
        row_blk = pl.BlockSpec((TM_CMB, d), lambda m: (m, 0))
        in_specs = [row_blk, row_blk, pl.BlockSpec((TM_CMB, d), lambda m: (m, 1)),
                    pl.BlockSpec((TM_CMB, LANES), lambda m: (m, 0)), mod_spec(5, TM_CMB)]
        args = [x, ypair, ypair, gw, mt]
        if with_next:
            in_specs += [full((1, d)), mod_spec(0, TM_CMB), mod_spec(1, TM_CMB)]
            args += [norm1_g[l + 1:l + 2], modtab[l + 1], modtab[l + 1]]
            out_shape = (jax.ShapeDtypeStruct((t, d), F32), jax.ShapeDtypeStruct((t, d), BF16))
            out_specs = (row_blk, row_blk)
        else:
            out_shape = (jax.ShapeDtypeStruct((t, d), F32),)
            out_specs = (row_blk,)
        res = pl.pallas_call(
            functools.partial(_combine_kernel, with_next),
            out_shape=out_shape, grid=(t // TM_CMB,), in_specs=in_specs, out_specs=out_specs,
            compiler_params=_cp(1),
            name="moe_combine",
        )(*args)
        x = res[0]
        if with_next:
            h = res[1]

    return (x[:tp].reshape(batch, seq, d), x[tp:].reshape(dec_batch, dec_seq, d),
            jnp.stack(a_k, axis=1), jnp.stack(a_v, axis=1), jnp.stack(c_kv, axis=1),
            jnp.stack(c_pe, axis=1), jnp.stack(d_k, axis=1), jnp.stack(d_v, axis=1))
```

```python
import functools

import jax
import jax.numpy as jnp
import numpy as np
from jax import lax
from jax.experimental import pallas as pl
from jax.experimental.pallas import tpu as pltpu

F32 = jnp.float32
BF16 = jnp.bfloat16

HEAD = 128
LANES = 128
EPS = 1e-6
ROPE_THETA = 10000.0
GRID_W = 64
NEG_INF = -1e30
N_MOD = 6
TOP_K = 2
N_GROUPS = 4
EXPERTS_PER_GROUP = 4
N_EXPERTS = N_GROUPS * EXPERTS_PER_GROUP
WINDOW = 128
VMEM_LIMIT = 56 * 1024 * 1024

TM = 1024
TM_OUT = 512
TN_OUT = 512
TM_CMB = 256
TB = 256
TN_EVEN = 768
TN_ODD = 1024
MOD_TN = 1024
MOD_ROWS = 16


def _cp(n_axes, vmem=VMEM_LIMIT):
    return pltpu.CompilerParams(dimension_semantics=("arbitrary",) * n_axes,
                                vmem_limit_bytes=vmem)


def _silu(x):
    return x * (1.0 / (1.0 + jnp.exp(-x)))


def _rms_scale(sumsq, n):
    return lax.rsqrt(sumsq * (1.0 / n) + EPS)


def _rope(y, cos, sin, q):
    lane = lax.broadcasted_iota(jnp.int32, y.shape, 1)
    first = jnp.bitwise_and(lane, 2 * q - 1) < q
    sw = jnp.where(first, pltpu.roll(y, LANES - q, 1), pltpu.roll(y, q, 1))
    return y * cos + sw * sin


def _rope_tables(n_tok, rot_dim):
    rows = n_tok // GRID_W
    row_id = jnp.repeat(jnp.arange(rows, dtype=F32), GRID_W)
    col_id = jnp.tile(jnp.arange(GRID_W, dtype=F32), rows)
    axis_dim = rot_dim // 2
    inv_freq = ROPE_THETA ** (-jnp.arange(0, axis_dim, 2, dtype=F32) / axis_dim)
    ang_r = row_id[:, None] * inv_freq[None, :]
    ang_c = col_id[:, None] * inv_freq[None, :]
    cr, sr, cc, sc = jnp.cos(ang_r), jnp.sin(ang_r), jnp.cos(ang_c), jnp.sin(ang_c)
    cos = jnp.concatenate([cr, cr, cc, cc], axis=-1)
    sin = jnp.concatenate([-sr, sr, -sc, sc], axis=-1)
    pad = LANES - rot_dim
    if pad:
        cos = jnp.pad(cos, ((0, 0), (0, pad)))
        sin = jnp.pad(sin, ((0, 0), (0, pad)))
    return cos, sin


def _mod_kernel(c_ref, w_ref, b_ref, o_ref):
    s = _silu(c_ref[...]).astype(BF16)
    o_ref[...] = jnp.dot(s, w_ref[...].astype(BF16), preferred_element_type=F32) + b_ref[...]


def _modulation(cvec, w_mod, b_mod):
    depth, d, n = w_mod.shape
    return pl.pallas_call(
        _mod_kernel,
        out_shape=jax.ShapeDtypeStruct((depth, MOD_ROWS, n), F32),
        grid=(depth, n // MOD_TN),
        in_specs=[pl.BlockSpec((MOD_ROWS, d), lambda l, j: (0, 0)),
                  pl.BlockSpec((None, d, MOD_TN), lambda l, j: (l, 0, j)),
                  pl.BlockSpec((None, 1, MOD_TN), lambda l, j: (l, 0, j))],
        out_specs=pl.BlockSpec((None, MOD_ROWS, MOD_TN), lambda l, j: (l, 0, j)),
        compiler_params=_cp(2),
        name="modulation",
    )(cvec, w_mod, b_mod.reshape(depth, 1, n))


def _adaln(x, g, shift, scale):
    ss = jnp.sum(x * x, axis=-1, keepdims=True)
    y = (x * _rms_scale(ss, x.shape[-1])) * g
    return y * (1.0 + scale) + shift


def _pre_kernel(npb, xp_ref, xs_ref, g_ref, sh_ref, sc_ref, x_ref, h_ref):
    def emit(src_ref):
        x = src_ref[...]
        x_ref[...] = x
        h_ref[...] = _adaln(x, g_ref[...], sh_ref[0], sc_ref[0]).astype(BF16)

    @pl.when(pl.program_id(0) < npb)
    def _():
        emit(xp_ref)

    @pl.when(pl.program_id(0) >= npb)
    def _():
        emit(xs_ref)


def _conv_gate(u, gb, gc, wc_ref, seq):
    z = gc * u
    rows = z.shape[0]
    pos = jnp.bitwise_and(lax.broadcasted_iota(jnp.int32, z.shape, 0), seq - 1)
    zp = jnp.where(pos == 0, 0.0, pltpu.roll(z, 1, 0))
    zn = jnp.where(pos == seq - 1, 0.0, pltpu.roll(z, rows - 1, 0))
    w = wc_ref[...]
    y = zp * w[0:1, :] + z * w[1:2, :] + zn * w[2:3, :]
    return gb * y


def _in_even_kernel(npb, seq_p, seq_s, n_qkv, n_normed, n_kvf0,
                    h_ref, wa_ref, wb_ref, wc3_ref, gcol_ref, cos_ref, sin_ref, wc_ref,
                    qkv_ref, kvf_ref, b_ref):
    m = pl.program_id(0)
    n = pl.program_id(1)
    h = h_ref[...]
    accs = [jnp.dot(h, w[...], preferred_element_type=F32) for w in (wa_ref, wb_ref, wc3_ref)]
    piece = accs[0].shape[1]
    per = piece // HEAD
    heads = 3 * per

    def qkv_epilogue(sample):
        ys = []
        for j in range(heads):
            sl = slice(j * HEAD, (j + 1) * HEAD)
            a = accs[j // per][:, (j % per) * HEAD:(j % per + 1) * HEAD]
            normed = n * heads + j < n_normed
            ss = jnp.sum(a * a, axis=-1, keepdims=True)
            y = (a * _rms_scale(ss, HEAD)) * gcol_ref[:, sl]
            if sample:
                y = _rope(y, cos_ref[...], sin_ref[...], HEAD // 4)
            y = jnp.where(normed, y, a)
            qkv_ref[:, sl] = y.astype(BF16)
            ys.append(y)

        @pl.when(n >= n_kvf0)
        def _():
            for j in range(heads):
                kvf_ref[:, j * HEAD:(j + 1) * HEAD] = ys[j]

    @pl.when(jnp.logical_and(n < n_qkv, m < npb))
    def _():
        qkv_epilogue(False)

    @pl.when(jnp.logical_and(n < n_qkv, m >= npb))
    def _():
        qkv_epilogue(True)

    @pl.when(jnp.logical_and(n >= n_qkv, m < npb))
    def _():
        b_ref[...] = _conv_gate(accs[0], accs[1], accs[2], wc_ref, seq_p).astype(BF16)

    @pl.when(jnp.logical_and(n >= n_qkv, m >= npb))
    def _():
        b_ref[...] = _conv_gate(accs[0], accs[1], accs[2], wc_ref, seq_s).astype(BF16)


def _in_odd_kernel(npb, kv_lora,
                   h_ref, w_ref, gcol_ref, cos_ref, sin_ref,
                   cqn_ref, o1b_ref, o1f_ref, qd_ref, o4b_ref, o4f_ref):
    m = pl.program_id(0)
    n = pl.program_id(1)
    acc = jnp.dot(h_ref[...], w_ref[...], preferred_element_type=F32)
    width = acc.shape[1]

    def head_norm(a, sl, sample):
        ss = jnp.sum(a * a, axis=-1, keepdims=True)
        y = (a * _rms_scale(ss, HEAD)) * gcol_ref[:, sl]
        if sample:
            y = _rope(y, cos_ref[...], sin_ref[...], HEAD // 4)
        return y

    @pl.when(n == 0)
    def _():
        ss = jnp.sum(acc * acc, axis=-1, keepdims=True)
        cqn_ref[...] = ((acc * _rms_scale(ss, width)) * gcol_ref[...]).astype(BF16)

    def block1(sample):
        a = acc[:, :kv_lora]
        ss = jnp.sum(a * a, axis=-1, keepdims=True)
        y = (a * _rms_scale(ss, kv_lora)) * gcol_ref[:, :kv_lora]
        o1f_ref[:, :kv_lora] = y
        o1b_ref[:, :kv_lora] = y.astype(BF16)
        for j in range(kv_lora // HEAD, width // HEAD):
            sl = slice(j * HEAD, (j + 1) * HEAD)
            y = head_norm(acc[:, sl], sl, sample)
            o1f_ref[:, sl] = y
            o1b_ref[:, sl] = y.astype(BF16)

    def blockq(sample):
        for j in range(width // HEAD):
            sl = slice(j * HEAD, (j + 1) * HEAD)
            qd_ref[:, sl] = head_norm(acc[:, sl], sl, sample).astype(BF16)

    @pl.when(jnp.logical_and(n == 1, m < npb))
    def _():
        block1(False)

    @pl.when(jnp.logical_and(n == 1, m >= npb))
    def _():
        block1(True)

    is_q = jnp.logical_or(n == 2, n == 3)

    @pl.when(jnp.logical_and(is_q, m < npb))
    def _():
        blockq(False)

    @pl.when(jnp.logical_and(is_q, m >= npb))
    def _():
        blockq(True)

    @pl.when(n == 4)
    def _():
        o4f_ref[...] = acc
        o4b_ref[...] = acc.astype(BF16)


def _qb_kernel(npb, n_heads, qk_head,
               x_ref, w_ref, g_ref, cos_ref, sin_ref, q_ref):
    m = pl.program_id(0)
    acc = jnp.dot(x_ref[...], w_ref[...], preferred_element_type=F32)

    def body(sample):
        for h in range(n_heads):
            s0 = slice(2 * h * HEAD, (2 * h + 1) * HEAD)
            s1 = slice((2 * h + 1) * HEAD, (2 * h + 2) * HEAD)
            nope, pe = acc[:, s0], acc[:, s1]
            ss = (jnp.sum(nope * nope, axis=-1, keepdims=True)
                  + jnp.sum(pe * pe, axis=-1, keepdims=True))
            r = _rms_scale(ss, qk_head)
            q_ref[:, s0] = ((nope * r) * g_ref[:, s0]).astype(BF16)
            y = (pe * r) * g_ref[:, s1]
            if sample:
                y = _rope(y, cos_ref[...], sin_ref[...], HEAD // 8)
            q_ref[:, s1] = y.astype(BF16)

    @pl.when(m < npb)
    def _():
        body(False)

    @pl.when(m >= npb)
    def _():
        body(True)


def _kvb_kernel(lo_s, hi_s, n_heads, qk_head,
                x_ref, pe_ref, w_ref, g_ref, cos_ref, sin_ref, k_ref, v_ref):
    m = pl.program_id(0)
    acc = jnp.dot(x_ref[...], w_ref[...], preferred_element_type=F32)
    kpe = pe_ref[...]
    ss_pe = jnp.sum(kpe * kpe, axis=-1, keepdims=True)
    nk = n_heads * HEAD

    def body(sample):
        pe_g = kpe * g_ref[:, HEAD:2 * HEAD]
        if sample:
            pe_g = _rope(pe_g, cos_ref[...], sin_ref[...], HEAD // 8)
        for h in range(n_heads):
            kn = acc[:, h * HEAD:(h + 1) * HEAD]
            ss = jnp.sum(kn * kn, axis=-1, keepdims=True) + ss_pe
            r = _rms_scale(ss, qk_head)
            k_ref[:, 2 * h * HEAD:(2 * h + 1) * HEAD] = ((kn * r) * g_ref[:, :HEAD]).astype(BF16)
            k_ref[:, (2 * h + 1) * HEAD:(2 * h + 2) * HEAD] = (pe_g * r).astype(BF16)
            v_ref[:, h * HEAD:(h + 1) * HEAD] = acc[:, nk + h * HEAD:nk + (h + 1) * HEAD].astype(BF16)

    is_s = jnp.logical_and(m >= lo_s, m < hi_s)

    @pl.when(jnp.logical_not(is_s))
    def _():
        body(False)

    @pl.when(is_s)
    def _():
        body(True)


def _softmax_pv(s, v, sink=None):
    mx = jnp.max(s, axis=-1, keepdims=True)
    if sink is not None:
        mx = jnp.maximum(mx, sink)
    p = jnp.exp(s - mx)
    l = jnp.sum(p, axis=-1, keepdims=True)
    if sink is not None:
        l = l + jnp.exp(sink - mx)
    p = p * (1.0 / l)
    return jnp.dot(p.astype(BF16), v, preferred_element_type=F32)


def _qk(q, k):
    return lax.dot_general(q, k, (((1,), (1,)), ((), ())), preferred_element_type=F32)


def _attn_prompt_kernel(n_seq, seq, groups, dk, scale, use_sink, *refs):
    if use_sink:
        sink_ref, q_ref, k_ref, v_ref, o_ref = refs
    else:
        q_ref, k_ref, v_ref, o_ref = refs
    j = pl.program_id(1)
    dv = v_ref.shape[1]
    for b in range(n_seq):
        rows = slice(b * seq, (b + 1) * seq)
        k = k_ref[rows, :]
        v = v_ref[rows, :]
        for g in range(groups):
            q = q_ref[rows, g * dk:(g + 1) * dk]
            s = _qk(q, k) * scale
            sink = sink_ref[j * groups + g] if use_sink else None
            o_ref[rows, g * dv:(g + 1) * dv] = _softmax_pv(s, v, sink).astype(o_ref.dtype)


def _attn_sample_kernel(seq, qblk, groups, dk, scale, *refs):
    q_ref, k_ref, v_ref, ck_ref, cv_ref, a_ref, o_ref = refs
    del a_ref
    dv = v_ref.shape[1]
    k = jnp.concatenate([k_ref[...], ck_ref[...].astype(BF16)], axis=0)
    v = jnp.concatenate([v_ref[...], cv_ref[...].astype(BF16)], axis=0)

    def step(i, carry):
        r0 = pl.multiple_of(i * qblk, qblk)
        for g in range(groups):
            q = q_ref[pl.ds(r0, qblk), g * dk:(g + 1) * dk]
            s = _qk(q, k) * scale
            o_ref[pl.ds(r0, qblk), g * dv:(g + 1) * dv] = _softmax_pv(s, v).astype(o_ref.dtype)
        return carry

    lax.fori_loop(0, seq // qblk, step, 0)


def _attn_window_kernel(seq, groups, scale, sink_ref, q_ref, k_ref, v_ref, ck_ref, cv_ref,
                        a_ref, o_ref):
    del a_ref
    j = pl.program_id(1)
    ck = ck_ref[...].astype(BF16)
    cv = cv_ref[...].astype(BF16)
    w = WINDOW
    rid = lax.shift_right_logical(lax.broadcasted_iota(jnp.int32, (groups * w, 1), 0),
                                  int(np.log2(w)))
    sink = jnp.zeros((groups * w, 1), F32)
    for g in range(groups):
        sink = jnp.where(rid == g, sink_ref[j * groups + g], sink)
    for qb in range(seq // w):
        lo = max(0, (qb - 1) * w)
        hi = min(seq, (qb + 2) * w)
        rows = slice(qb * w, (qb + 1) * w)
        q = jnp.concatenate([q_ref[rows, g * HEAD:(g + 1) * HEAD] for g in range(groups)], axis=0)
        kl = k_ref[lo:hi, :]
        vl = v_ref[lo:hi, :]
        s_loc = _qk(q, kl) * scale
        qpos = qb * w + jnp.bitwise_and(lax.broadcasted_iota(jnp.int32, s_loc.shape, 0), w - 1)
        kpos = lo + lax.broadcasted_iota(jnp.int32, s_loc.shape, 1)
        s_loc = jnp.where(jnp.abs(qpos - kpos) <= w, s_loc, NEG_INF)
        s_ctx = _qk(q, ck) * scale
        mx = jnp.maximum(jnp.maximum(jnp.max(s_loc, axis=-1, keepdims=True),
                                     jnp.max(s_ctx, axis=-1, keepdims=True)), sink)
        p_loc = jnp.exp(s_loc - mx)
        p_ctx = jnp.exp(s_ctx - mx)
        l = (jnp.sum(p_loc, axis=-1, keepdims=True) + jnp.sum(p_ctx, axis=-1, keepdims=True)
             + jnp.exp(sink - mx))
        inv = 1.0 / l
        o = (jnp.dot((p_loc * inv).astype(BF16), vl, preferred_element_type=F32)
             + jnp.dot((p_ctx * inv).astype(BF16), cv, preferred_element_type=F32))
        for g in range(groups):
            o_ref[rows, g * HEAD:(g + 1) * HEAD] = o[g * w:(g + 1) * w, :].astype(o_ref.dtype)


def _out_kernel(n_chunks, a_ref, b_ref, w_ref, x_ref, gate_ref, g2_ref, sh_ref, sc_ref,
                wr_ref, rb_ref, xo_ref, h2_ref, lg_ref, xs_ref):
    n = pl.program_id(1)
    ka = a_ref.shape[1]
    acc = (jnp.dot(a_ref[...], w_ref[:ka, :], preferred_element_type=F32)
           + jnp.dot(b_ref[...], w_ref[ka:, :], preferred_element_type=F32))
    xn = x_ref[...] + gate_ref[0] * acc
    xo_ref[...] = xn
    xs_ref[n] = xn

    @pl.when(n == n_chunks - 1)
    def _():
        tn = xn.shape[1]
        ss = jnp.zeros((xn.shape[0], 1), F32)
        for c in range(n_chunks):
            xc = xs_ref[c]
            ss = ss + jnp.sum(xc * xc, axis=-1, keepdims=True)
        r = _rms_scale(ss, n_chunks * tn)
        lg = jnp.zeros(lg_ref.shape, F32) + rb_ref[...]
        for c in range(n_chunks):
            sl = slice(c * tn, (c + 1) * tn)
            y = (xs_ref[c] * r) * g2_ref[:, sl]
            h = y * (1.0 + sc_ref[0][:, sl]) + sh_ref[0][:, sl]
            hi = h.astype(BF16)
            lo = (h - hi.astype(F32)).astype(BF16)
            h2_ref[:, sl] = hi
            both = jnp.dot(hi, wr_ref[sl, :], preferred_element_type=F32)
            lg = lg + (both[:, :LANES] + both[:, LANES:]
                       + jnp.dot(lo, wr_ref[sl, :LANES], preferred_element_type=F32))
        lg_ref[...] = lg


def _block_pipeline(nblk, in_copies, out_copy, compute):
    @pl.when(nblk > 0)
    def _():
        for cp in in_copies(0, 0):
            cp.start()

        def body(i, carry):
            slot = jnp.bitwise_and(i, 1)

            @pl.when(i + 1 < nblk)
            def _():
                for cp in in_copies(i + 1, 1 - slot):
                    cp.start()

            for cp in in_copies(i, slot):
                cp.wait()

            @pl.when(i >= 2)
            def _():
                out_copy(i - 2, slot).wait()

            compute(slot)
            out_copy(i, slot).start()
            return carry

        lax.fori_loop(0, nblk, body, 0)

        @pl.when(nblk >= 2)
        def _():
            out_copy(nblk - 2, jnp.bitwise_and(nblk, 1)).wait()

        out_copy(nblk - 1, jnp.bitwise_and(nblk - 1, 1)).wait()


def _up_kernel(b0_ref, nb_ref, x_hbm, wg_ref, wu_ref, a_hbm, wg_s, wu_s, xbuf, obuf, sem_in, sem_out):
    cc = pl.program_id(0)
    e = pl.program_id(1)
    first = b0_ref[e]

    def rows(i):
        return pl.ds(pl.multiple_of((first + i) * TB, TB), TB)

    def in_copies(i, slot):
        return [pltpu.make_async_copy(x_hbm.at[rows(i), :], xbuf.at[slot], sem_in.at[slot])]

    def out_copy(i, slot):
        return pltpu.make_async_copy(obuf.at[slot], a_hbm.at[cc, rows(i), :], sem_out.at[slot])

    def compute(slot):
        x = xbuf[slot]
        g = jnp.dot(x, wg_s[...], preferred_element_type=F32)
        u = jnp.dot(x, wu_s[...], preferred_element_type=F32)
        obuf[slot] = (_silu(g) * u).astype(BF16)

    wg_s[...] = wg_ref[...].astype(BF16)
    wu_s[...] = wu_ref[...].astype(BF16)
    _block_pipeline(nb_ref[e], in_copies, out_copy, compute)


def _down_kernel(b0_ref, nb_ref, a_hbm, wd_ref, y_hbm, wd_s, abuf, ybuf, sem_in, sem_out):
    e = pl.program_id(0)
    first = b0_ref[e]
    n_half, _, fc = a_hbm.shape

    def rows(i):
        return pl.ds(pl.multiple_of((first + i) * TB, TB), TB)

    def in_copies(i, slot):
        return [pltpu.make_async_copy(a_hbm.at[k, rows(i), :],
                                      abuf.at[slot, :, k * fc:(k + 1) * fc], sem_in.at[slot, k])
                for k in range(n_half)]

    def out_copy(i, slot):
        return pltpu.make_async_copy(ybuf.at[slot], y_hbm.at[rows(i), :], sem_out.at[slot])

    def compute(slot):
        ybuf[slot] = jnp.dot(abuf[slot], wd_s[...], preferred_element_type=F32)

    wd_s[...] = wd_ref[...].astype(BF16)
    _block_pipeline(nb_ref[e], in_copies, out_copy, compute)


def _combine_kernel(with_next, npb, tbl_ref, tbl_next_ref, x_ref, y_hbm, gw_ref, gate_ref, *refs):
    if with_next:
        g_ref, sh_ref, sc_ref, xo_ref, h_ref, ybuf, sem = refs
    else:
        xp_ref, xs_ref, ybuf, sem = refs
    m = pl.program_id(0)
    n_steps = pl.num_programs(0)
    n_rows = ybuf.shape[1]
    tm = n_rows // TOP_K
    slot = jnp.bitwise_and(m, 1)

    def row_copy(tbl, r, s):
        return pltpu.make_async_copy(y_hbm.at[pl.ds(tbl[0, r], 1), :], ybuf.at[s, pl.ds(r, 1), :],
                                     sem.at[s])

    def start_all(tbl, s):
        def body(r, carry):
            row_copy(tbl, r, s).start()
            return carry
        lax.fori_loop(0, n_rows, body, 0, unroll=8)

    def wait_all(tbl, s):
        def body(r, carry):
            row_copy(tbl, r, s).wait()
            return carry
        lax.fori_loop(0, n_rows, body, 0, unroll=8)

    @pl.when(m == 0)
    def _():
        start_all(tbl_ref, 0)

    @pl.when(m + 1 < n_steps)
    def _():
        start_all(tbl_next_ref, 1 - slot)

    wait_all(tbl_ref, slot)
    gw = gw_ref[...]
    yb = ybuf.at[slot]
    moe = gw[:, 0:1] * yb[:tm, :] + gw[:, 1:2] * yb[tm:, :]
    xn = x_ref[...] + gate_ref[0] * moe
    if with_next:
        xo_ref[...] = xn
        h_ref[...] = _adaln(xn, g_ref[...], sh_ref[0], sc_ref[0]).astype(BF16)
    else:
        @pl.when(m < npb)
        def _():
            xp_ref[...] = xn

        @pl.when(m >= npb)
        def _():
            xs_ref[...] = xn


def kernel(x_prompt, x_sample, cache_a_k, cache_a_v, cache_c_kv, cache_c_pe, cache_d_k, cache_d_v,
           c, c_ctx, w_mod, b_mod, norm1_g, norm2_g, w_in_even, w_out_even, a_q_norm, a_k_norm,
           b_conv, w_in_odd, w_out_odd, c_q_a_norm, c_kv_a_norm, c_w_q_b, c_w_kv_b, c_q_norm,
           c_k_norm, d_q_norm, d_k_norm, d_sink, w_group_router, b_group_router, w_expert_router,
           b_expert_router, w_expert_gate, w_expert_up, w_expert_down):
    batch, seq, d = x_prompt.shape
    dec_batch, dec_seq, _ = x_sample.shape
    past = cache_a_k.shape[2]
    depth = w_mod.shape[0]
    kvh = cache_a_k.shape[3]
    b_width = d // 2
    a_width = w_out_even.shape[1] - b_width
    a_heads = a_width // HEAD
    groups = a_heads // kvh
    kv_width = kvh * HEAD
    q_lora = c_q_a_norm.shape[1]
    kv_lora = c_kv_a_norm.shape[1]
    qk_head = c_q_norm.shape[1]
    qk_rope = cache_c_pe.shape[3]
    qk_nope = qk_head - qk_rope
    c_heads = c_w_q_b.shape[2] // qk_head
    v_dim = c_w_kv_b.shape[2] // c_heads - qk_nope
    d_width = d_sink.shape[1] * HEAD
    d_expert = w_expert_gate.shape[3]
    tp = batch * seq
    ts = dec_batch * dec_seq
    t = tp + ts
    assert qk_nope == HEAD and v_dim == HEAD and 2 * qk_rope == HEAD and d_width == a_width
    assert seq == past and dec_seq == TM and tp % TM == 0 and TM % seq == 0

    def row_of(tm):
        npb, bps = tp // tm, dec_seq // tm
        return lambda m: jnp.where(m < npb, 0, 1 + (m - npb) // bps)

    cvec = jnp.zeros((MOD_ROWS, d), F32).at[0].set(c_ctx).at[1:1 + dec_batch].set(c)
    mods = _modulation(cvec, w_mod, b_mod)
    n_rows = 1 + dec_batch
    modtab = (mods[:, :n_rows].reshape(depth, n_rows, N_MOD, d).transpose(0, 2, 1, 3)
              .reshape(depth, N_MOD * n_rows, 1, d))

    def mod_spec(j, tm, cols=None, col_axis=None):
        r = row_of(tm)
        if cols is None:
            return pl.BlockSpec((1, 1, d), lambda *g: (j * n_rows + r(g[0]), 0, 0))
        return pl.BlockSpec((1, 1, cols), lambda *g: (j * n_rows + r(g[0]), 0, g[col_axis]))

    cos_hd, sin_hd = _rope_tables(dec_seq, HEAD)
    cos_pe, sin_pe = _rope_tables(dec_seq, qk_rope)

    npb = tp // TM
    npb_o = tp // TM_CMB

    full = lambda shape: pl.BlockSpec(shape, lambda *g: (0,) * len(shape))

    x, h = pl.pallas_call(
        functools.partial(_pre_kernel, npb_o),
        out_shape=(jax.ShapeDtypeStruct((t, d), F32), jax.ShapeDtypeStruct((t, d), BF16)),
        grid=(t // TM_CMB,),
        in_specs=[pl.BlockSpec((TM_CMB, d), lambda m: (jnp.minimum(m, npb_o - 1), 0)),
                  pl.BlockSpec((TM_CMB, d), lambda m: (jnp.maximum(m - npb_o, 0), 0)),
                  full((1, d)), mod_spec(0, TM_CMB), mod_spec(1, TM_CMB)],
        out_specs=(pl.BlockSpec((TM_CMB, d), lambda m: (m, 0)),
                   pl.BlockSpec((TM_CMB, d), lambda m: (m, 0))),
        compiler_params=_cp(1),
        name="adaln_first",
    )(x_prompt.reshape(tp, d), x_sample.reshape(ts, d), norm1_g[0:1], modtab[0], modtab[0])

    a_k, a_v, c_kv, c_pe, d_k, d_v = [], [], [], [], [], []

    for l in range(depth):
        i = l // 2
        mt = modtab[l]
        if l % 2 == 0:
            w_in = w_in_even[i].astype(BF16)
            qkv_w = a_width + 2 * kv_width
            cch = TN_EVEN // 3
            n_qkv = qkv_w // TN_EVEN
            pb = qkv_w // cch
            nch = b_width // cch
            n_kvf0 = a_width // TN_EVEN
            kvf_off = a_width - n_kvf0 * TN_EVEN
            gcol = jnp.concatenate([jnp.tile(a_q_norm[i], a_heads), jnp.tile(a_k_norm[i], kvh),
                                    jnp.ones((kv_width,), F32)])[None, :]
            n_in = n_qkv + nch

            def piece(k):
                return pl.BlockSpec((d, cch), lambda m, n: (
                    0, jnp.where(n < n_qkv, 3 * n + k, pb + k * nch + n - n_qkv)))

            qkv, kvf, bconv = pl.pallas_call(
                functools.partial(_in_even_kernel, npb, seq, dec_seq, n_qkv, a_heads + kvh, n_kvf0),
                out_shape=(jax.ShapeDtypeStruct((t, qkv_w), BF16),
                           jax.ShapeDtypeStruct((t, (n_qkv - n_kvf0) * TN_EVEN), F32),
                           jax.ShapeDtypeStruct((t, b_width), BF16)),
                grid=(t // TM, n_in),
                in_specs=[pl.BlockSpec((TM, d), lambda m, n: (m, 0)),
                          piece(0), piece(1), piece(2),
                          pl.BlockSpec((1, TN_EVEN), lambda m, n: (0, jnp.minimum(n, n_qkv - 1))),
                          full((TM, LANES)), full((TM, LANES)),
                          pl.BlockSpec((3, cch), lambda m, n: (0, jnp.maximum(n - n_qkv, 0)))],
                out_specs=(pl.BlockSpec((TM, TN_EVEN), lambda m, n: (m, jnp.minimum(n, n_qkv - 1))),
                           pl.BlockSpec((TM, TN_EVEN),
                                        lambda m, n: (m, jnp.clip(n - n_kvf0, 0, n_qkv - n_kvf0 - 1))),
                           pl.BlockSpec((TM, cch), lambda m, n: (m, jnp.maximum(n - n_qkv, 0)))),
                compiler_params=_cp(2),
                name="in_proj_even",
            )(h, w_in, w_in, w_in, gcol, cos_hd, sin_hd, b_conv[i])
            a_k.append(kvf[:tp, kvf_off:kvf_off + kv_width].reshape(batch, seq, kvh, HEAD))
            a_v.append(kvf[:tp, kvf_off + kv_width:kvf_off + 2 * kv_width]
                       .reshape(batch, seq, kvh, HEAD))

            gw_ = groups * HEAD
            qb0, kb, vb = 0, a_width // HEAD, (a_width + kv_width) // HEAD
            scale = HEAD ** -0.5
            att = pl.pallas_call(
                functools.partial(_attn_prompt_kernel, TM // seq, seq, groups, HEAD, scale, False),
                out_shape=jax.ShapeDtypeStruct((t, a_width), BF16),
                grid=(npb, kvh),
                in_specs=[pl.BlockSpec((TM, gw_), lambda m, j: (m, qb0 + j)),
                          pl.BlockSpec((TM, HEAD), lambda m, j: (m, kb + j)),
                          pl.BlockSpec((TM, HEAD), lambda m, j: (m, vb + j))],
                out_specs=pl.BlockSpec((TM, gw_), lambda m, j: (m, j)),
                compiler_params=_cp(2),
                name="attn_a_prompt",
            )(qkv, qkv, qkv)
            ck = cache_a_k[:, i].reshape(dec_batch, past, kv_width)
            cv = cache_a_v[:, i].reshape(dec_batch, past, kv_width)
            att = pl.pallas_call(
                functools.partial(_attn_sample_kernel, dec_seq, 256, groups, HEAD, scale),
                out_shape=jax.ShapeDtypeStruct((t, a_width), BF16),
                grid=(dec_batch, kvh),
                in_specs=[pl.BlockSpec((TM, gw_), lambda b, j: (npb + b, qb0 + j)),
                          pl.BlockSpec((TM, HEAD), lambda b, j: (npb + b, kb + j)),
                          pl.BlockSpec((TM, HEAD), lambda b, j: (npb + b, vb + j)),
                          pl.BlockSpec((None, past, HEAD), lambda b, j: (b, 0, j)),
                          pl.BlockSpec((None, past, HEAD), lambda b, j: (b, 0, j)),
                          pl.BlockSpec(memory_space=pl.ANY)],
                out_specs=pl.BlockSpec((TM, gw_), lambda b, j: (npb + b, j)),
                input_output_aliases={5: 0},
                compiler_params=_cp(2),
                name="attn_a_sample",
            )(qkv, qkv, qkv, ck, cv, att)
            lhs_a, lhs_b = att, bconv
            w_out = w_out_even[i].astype(BF16)
        else:
            w = w_in_odd[i]
            o = np.cumsum([0, q_lora, kv_lora, qk_rope, d_width, kv_width, kv_width])
            cq_w, ckv_w, kpe_w, dq_w, dk_w, dv_w = [w[:, o[s]:o[s + 1]] for s in range(6)]
            tail = TN_ODD - kv_width - qk_rope
            w_in = jnp.concatenate([cq_w, ckv_w, dk_w, dq_w, dv_w, kpe_w,
                                    jnp.zeros((d, tail), F32)], axis=1).astype(BF16)
            assert q_lora == TN_ODD and kv_lora + kv_width == TN_ODD and d_width == 2 * TN_ODD
            gcol = jnp.concatenate([c_q_a_norm[i], c_kv_a_norm[i], jnp.tile(d_k_norm[i], kvh),
                                    jnp.tile(d_q_norm[i], d_width // HEAD),
                                    jnp.ones((TN_ODD,), F32)])[None, :]
            n_in = w_in.shape[1] // TN_ODD
            blk = lambda fn: pl.BlockSpec((TM_OUT, TN_ODD), fn)
            spb = dec_seq // TM_OUT
            cqn, o1b, o1f, qd, o4b, o4f = pl.pallas_call(
                functools.partial(_in_odd_kernel, tp // TM_OUT, kv_lora),
                out_shape=(jax.ShapeDtypeStruct((t, TN_ODD), BF16),
                           jax.ShapeDtypeStruct((t, TN_ODD), BF16),
                           jax.ShapeDtypeStruct((t, TN_ODD), F32),
                           jax.ShapeDtypeStruct((t, d_width), BF16),
                           jax.ShapeDtypeStruct((t, TN_ODD), BF16),
                           jax.ShapeDtypeStruct((t, TN_ODD), F32)),
                grid=(t // TM_OUT, n_in),
                in_specs=[pl.BlockSpec((TM_OUT, d), lambda m, n: (m, 0)),
                          pl.BlockSpec((d, TN_ODD), lambda m, n: (0, n)),
                          pl.BlockSpec((1, TN_ODD), lambda m, n: (0, n)),
                          pl.BlockSpec((TM_OUT, LANES), lambda m, n: (m % spb, 0)),
                          pl.BlockSpec((TM_OUT, LANES), lambda m, n: (m % spb, 0))],
                out_specs=(blk(lambda m, n: (m, 0)), blk(lambda m, n: (m, 0)),
                           blk(lambda m, n: (m, 0)),
                           blk(lambda m, n: (m, jnp.clip(n - 2, 0, 1))),
                           blk(lambda m, n: (m, 0)), blk(lambda m, n: (m, 0))),
                compiler_params=_cp(2),
                name="in_proj_odd",
            )(h, w_in, gcol, cos_hd, sin_hd)
            c_kv.append(o1f[:tp, :kv_lora].reshape(batch, seq, kv_lora))
            d_k.append(o1f[:tp, kv_lora:].reshape(batch, seq, kvh, HEAD))
            d_v.append(o4f[:tp, :kv_width].reshape(batch, seq, kvh, HEAD))
            c_pe.append(o4f[:tp, kv_width:kv_width + qk_rope].reshape(batch, seq, qk_rope))

            wq = c_w_q_b[i].reshape(q_lora, c_heads, qk_head)
            wq = jnp.pad(wq, ((0, 0), (0, 0), (0, 2 * HEAD - qk_head)))
            wq = wq.reshape(q_lora, c_heads * 2 * HEAD).astype(BF16)
            gq = jnp.tile(jnp.pad(c_q_norm[i], (0, 2 * HEAD - qk_head)), c_heads)[None, :]
            cw = c_heads * 2 * HEAD
            q_c = pl.pallas_call(
                functools.partial(_qb_kernel, tp // TM_OUT, c_heads, qk_head),
                out_shape=jax.ShapeDtypeStruct((t, cw), BF16),
                grid=(t // TM_OUT,),
                in_specs=[pl.BlockSpec((TM_OUT, q_lora), lambda m: (m, 0)),
                          full((q_lora, cw)), full((1, cw)),
                          pl.BlockSpec((TM_OUT, LANES), lambda m: (m % (dec_seq // TM_OUT), 0)),
                          pl.BlockSpec((TM_OUT, LANES), lambda m: (m % (dec_seq // TM_OUT), 0))],
                out_specs=pl.BlockSpec((TM_OUT, cw), lambda m: (m, 0)),
                compiler_params=_cp(1),
                name="mla_q_up",
            )(cqn, wq, gq, cos_pe, sin_pe)

            wkv = c_w_kv_b[i].reshape(kv_lora, c_heads, qk_nope + v_dim)
            wkv = jnp.concatenate([wkv[:, :, :qk_nope].reshape(kv_lora, c_heads * qk_nope),
                                   wkv[:, :, qk_nope:].reshape(kv_lora, c_heads * v_dim)],
                                  axis=1).astype(BF16)
            n_ctx = dec_batch * past
            ckv_all = jnp.concatenate([o1b[:, :kv_lora],
                                       cache_c_kv[:, i].reshape(n_ctx, kv_lora).astype(BF16)], axis=0)
            kpe_all = jnp.concatenate(
                [o4f[:, kv_width:kv_width + HEAD],
                 jnp.pad(cache_c_pe[:, i].reshape(n_ctx, qk_rope), ((0, 0), (0, HEAD - qk_rope)))],
                axis=0)
            gk = jnp.pad(c_k_norm[i], (0, 2 * HEAD - qk_head))[None, :]
            t2 = t + n_ctx
            spb = dec_seq // TM_OUT
            k_c, v_c = pl.pallas_call(
                functools.partial(_kvb_kernel, tp // TM_OUT, t // TM_OUT, c_heads, qk_head),
                out_shape=(jax.ShapeDtypeStruct((t2, cw), BF16),
                           jax.ShapeDtypeStruct((t2, c_heads * v_dim), BF16)),
                grid=(t2 // TM_OUT,),
                in_specs=[pl.BlockSpec((TM_OUT, kv_lora), lambda m: (m, 0)),
                          pl.BlockSpec((TM_OUT, HEAD), lambda m: (m, 0)),
                          full((kv_lora, c_heads * (qk_nope + v_dim))), full((1, 2 * HEAD)),
                          pl.BlockSpec((TM_OUT, LANES), lambda m: (m % spb, 0)),
                          pl.BlockSpec((TM_OUT, LANES), lambda m: (m % spb, 0))],
                out_specs=(pl.BlockSpec((TM_OUT, cw), lambda m: (m, 0)),
                           pl.BlockSpec((TM_OUT, c_heads * v_dim), lambda m: (m, 0))),
                compiler_params=_cp(1),
                name="mla_kv_up",
            )(ckv_all, kpe_all, wkv, gk, cos_pe, sin_pe)

            scale_c = qk_head ** -0.5
            oc = pl.pallas_call(
                functools.partial(_attn_prompt_kernel, TM // seq, seq, 1, 2 * HEAD, scale_c, False),
                out_shape=jax.ShapeDtypeStruct((t, c_heads * v_dim), BF16),
                grid=(npb, c_heads),
                in_specs=[pl.BlockSpec((TM, 2 * HEAD), lambda m, j: (m, j)),
                          pl.BlockSpec((TM, 2 * HEAD), lambda m, j: (m, j)),
                          pl.BlockSpec((TM, v_dim), lambda m, j: (m, j))],
                out_specs=pl.BlockSpec((TM, v_dim), lambda m, j: (m, j)),
                compiler_params=_cp(2),
                name="attn_c_prompt",
            )(q_c, k_c, v_c)
            cb0 = t // past
            oc = pl.pallas_call(
                functools.partial(_attn_sample_kernel, dec_seq, 256, 1, 2 * HEAD, scale_c),
                out_shape=jax.ShapeDtypeStruct((t, c_heads * v_dim), BF16),
                grid=(dec_batch, c_heads),
                in_specs=[pl.BlockSpec((TM, 2 * HEAD), lambda b, j: (npb + b, j)),
                          pl.BlockSpec((TM, 2 * HEAD), lambda b, j: (npb + b, j)),
                          pl.BlockSpec((TM, v_dim), lambda b, j: (npb + b, j)),
                          pl.BlockSpec((past, 2 * HEAD), lambda b, j: (cb0 + b, j)),
                          pl.BlockSpec((past, v_dim), lambda b, j: (cb0 + b, j)),
                          pl.BlockSpec(memory_space=pl.ANY)],
                out_specs=pl.BlockSpec((TM, v_dim), lambda b, j: (npb + b, j)),
                input_output_aliases={5: 0},
                compiler_params=_cp(2),
                name="attn_c_sample",
            )(q_c, k_c, v_c, k_c, v_c, oc)

            gw_ = groups * HEAD
            scale = HEAD ** -0.5
            sink = d_sink[i]
            kdb = kv_lora // HEAD
            od = pl.pallas_call(
                functools.partial(_attn_prompt_kernel, TM // seq, seq, groups, HEAD, scale, True),
                out_shape=jax.ShapeDtypeStruct((t, d_width), BF16),
                grid_spec=pltpu.PrefetchScalarGridSpec(
                    num_scalar_prefetch=1, grid=(npb, kvh),
                    in_specs=[pl.BlockSpec((TM, gw_), lambda m, j, s: (m, j)),
                              pl.BlockSpec((TM, HEAD), lambda m, j, s: (m, kdb + j)),
                              pl.BlockSpec((TM, HEAD), lambda m, j, s: (m, j))],
                    out_specs=pl.BlockSpec((TM, gw_), lambda m, j, s: (m, j))),
                compiler_params=_cp(2),
                name="attn_d_prompt",
            )(sink, qd, o1b, o4b)
            ck = cache_d_k[:, i].reshape(dec_batch, past, kv_width)
            cv = cache_d_v[:, i].reshape(dec_batch, past, kv_width)
            od = pl.pallas_call(
                functools.partial(_attn_window_kernel, dec_seq, groups, scale),
                out_shape=jax.ShapeDtypeStruct((t, d_width), BF16),
                grid_spec=pltpu.PrefetchScalarGridSpec(
                    num_scalar_prefetch=1, grid=(dec_batch, kvh),
                    in_specs=[pl.BlockSpec((TM, gw_), lambda b, j, s: (npb + b, j)),
                              pl.BlockSpec((TM, HEAD), lambda b, j, s: (npb + b, kdb + j)),
                              pl.BlockSpec((TM, HEAD), lambda b, j, s: (npb + b, j)),
                              pl.BlockSpec((None, past, HEAD), lambda b, j, s: (b, 0, j)),
                              pl.BlockSpec((None, past, HEAD), lambda b, j, s: (b, 0, j)),
                              pl.BlockSpec(memory_space=pl.ANY)],
                    out_specs=pl.BlockSpec((TM, gw_), lambda b, j, s: (npb + b, j))),
                input_output_aliases={6: 0},
                compiler_params=_cp(2),
                name="attn_d_sample",
            )(sink, qd, o1b, o4b, ck, cv, od)
            lhs_a, lhs_b = oc, od
            w_out = w_out_odd[i].astype(BF16)

        wr = jnp.concatenate([w_group_router[l], w_expert_router[l]], axis=1)
        n_r = wr.shape[1]
        wr = jnp.pad(wr, ((0, 0), (0, LANES - n_r)))
        wr_hi = wr.astype(BF16)
        wr_cat = jnp.concatenate([wr_hi, (wr - wr_hi.astype(F32)).astype(BF16)], axis=1)
        rb = jnp.pad(jnp.concatenate([b_group_router[l], b_expert_router[l]]), (0, LANES - n_r))[None]
        n_out = d // TN_OUT
        ka = lhs_a.shape[1]
        x, h2, logits = pl.pallas_call(
            functools.partial(_out_kernel, n_out),
            out_shape=(jax.ShapeDtypeStruct((t, d), F32), jax.ShapeDtypeStruct((t, d), BF16),
                       jax.ShapeDtypeStruct((t, LANES), F32)),
            grid=(t // TM_OUT, n_out),
            in_specs=[pl.BlockSpec((TM_OUT, ka), lambda m, n: (m, 0)),
                      pl.BlockSpec((TM_OUT, lhs_b.shape[1]), lambda m, n: (m, 0)),
                      pl.BlockSpec((w_out.shape[0], TN_OUT), lambda m, n: (0, n)),
                      pl.BlockSpec((TM_OUT, TN_OUT), lambda m, n: (m, n)),
                      mod_spec(2, TM_OUT, TN_OUT, 1),
                      full((1, d)), mod_spec(3, TM_OUT), mod_spec(4, TM_OUT),
                      full((d, 2 * LANES)), full((1, LANES))],
            out_specs=(pl.BlockSpec((TM_OUT, TN_OUT), lambda m, n: (m, n)),
                       pl.BlockSpec((TM_OUT, d), lambda m, n: (m, 0)),
                       pl.BlockSpec((TM_OUT, LANES), lambda m, n: (m, 0))),
            scratch_shapes=[pltpu.VMEM((n_out, TM_OUT, TN_OUT), F32)],
            compiler_params=_cp(2),
            name="out_proj",
        )(lhs_a, lhs_b, w_out, x, mt, norm2_g[l:l + 1], mt, mt, wr_cat, rb)

        g_logits = logits[:, :N_GROUPS]
        e_logits = logits[:, N_GROUPS:N_GROUPS + N_EXPERTS].reshape(t, N_GROUPS, EXPERTS_PER_GROUP)
        g_prob = jax.nn.softmax(g_logits, axis=-1)
        g_idx = jnp.argmax(g_logits, axis=-1)
        e_in = jnp.take_along_axis(e_logits, g_idx[:, None, None], axis=1)[:, 0]
        top_p, top_i = lax.top_k(jax.nn.softmax(e_in, axis=-1), TOP_K)
        gate = (jnp.take_along_axis(g_prob, g_idx[:, None], axis=1) * top_p
                / jnp.sum(top_p, axis=-1, keepdims=True))
        eid = (g_idx[:, None] * EXPERTS_PER_GROUP + top_i).reshape(-1).astype(jnp.int32)
        n_pairs = t * TOP_K
        onehot = (eid[:, None] == jnp.arange(N_EXPERTS, dtype=jnp.int32)[None, :]).astype(jnp.int32)
        rank = jnp.take_along_axis(jnp.cumsum(onehot, axis=0) - onehot, eid[:, None], axis=1)[:, 0]
        counts = jnp.sum(onehot, axis=0)
        nblk_e = (counts + TB - 1) // TB
        blk_end = jnp.cumsum(nblk_e)
        blk_start = blk_end - nblk_e
        dest = (blk_start[eid] * TB + rank).astype(jnp.int32)
        n_blocks = (n_pairs + N_EXPERTS * (TB - 1) + TB - 1) // TB
        n_slots = n_blocks * TB
        slot_token = jnp.zeros((n_slots,), jnp.int32).at[dest].set(
            jnp.arange(n_pairs, dtype=jnp.int32) // TOP_K)
        blk_start = blk_start.astype(jnp.int32)
        nblk_e = nblk_e.astype(jnp.int32)
        xs = h2[slot_token]

        n_half = 2
        fc = d_expert // n_half
        any_spec = pl.BlockSpec(memory_space=pl.ANY)
        act = pl.pallas_call(
            _up_kernel,
            out_shape=jax.ShapeDtypeStruct((n_half, n_slots, fc), BF16),
            grid_spec=pltpu.PrefetchScalarGridSpec(
                num_scalar_prefetch=2, grid=(n_half, N_EXPERTS),
                in_specs=[any_spec,
                          pl.BlockSpec((None, None, d, fc), lambda cc, e, b0, nb: (l, e, 0, cc)),
                          pl.BlockSpec((None, None, d, fc), lambda cc, e, b0, nb: (l, e, 0, cc))],
                out_specs=any_spec,
                scratch_shapes=[pltpu.VMEM((d, fc), BF16), pltpu.VMEM((d, fc), BF16),
                                pltpu.VMEM((2, TB, d), BF16), pltpu.VMEM((2, TB, fc), BF16),
                                pltpu.SemaphoreType.DMA((2,)), pltpu.SemaphoreType.DMA((2,))]),
            compiler_params=_cp(2),
            name="expert_up",
        )(blk_start, nblk_e, xs, w_expert_gate, w_expert_up)
        yb = pl.pallas_call(
            _down_kernel,
            out_shape=jax.ShapeDtypeStruct((n_slots, d), F32),
            grid_spec=pltpu.PrefetchScalarGridSpec(
                num_scalar_prefetch=2, grid=(N_EXPERTS,),
                in_specs=[any_spec,
                          pl.BlockSpec((None, None, d_expert, d), lambda e, b0, nb: (l, e, 0, 0))],
                out_specs=any_spec,
                scratch_shapes=[pltpu.VMEM((d_expert, d), BF16),
                                pltpu.VMEM((2, TB, d_expert), BF16), pltpu.VMEM((2, TB, d), F32),
                                pltpu.SemaphoreType.DMA((2, n_half)), pltpu.SemaphoreType.DMA((2,))]),
            compiler_params=_cp(1),
            name="expert_down",
        )(blk_start, nblk_e, act, w_expert_down)

        n_cmb = t // TM_CMB
        dest2 = dest.reshape(n_cmb, TM_CMB, TOP_K)
        tbl = jnp.concatenate([dest2[:, :, k] for k in range(TOP_K)], axis=1)[:, None, :]
        gw = jnp.pad(gate.astype(F32), ((0, 0), (0, LANES - TOP_K)))
        with_next = l + 1 < depth
        npb_c = tp // TM_CMB
        row_blk = pl.BlockSpec((TM_CMB, d), lambda m: (m, 0))
        smem_blk = lambda fn: pl.BlockSpec((None, 1, TOP_K * TM_CMB), fn, memory_space=pltpu.SMEM)
        in_specs = [smem_blk(lambda m: (m, 0, 0)),
                    smem_blk(lambda m: (jnp.minimum(m + 1, n_cmb - 1), 0, 0)),
                    row_blk, any_spec,
                    pl.BlockSpec((TM_CMB, LANES), lambda m: (m, 0)), mod_spec(5, TM_CMB)]
        args = [tbl, tbl, x, yb, gw, mt]
        if with_next:
            in_specs += [full((1, d)), mod_spec(0, TM_CMB), mod_spec(1, TM_CMB)]
            args += [norm1_g[l + 1:l + 2], modtab[l + 1], modtab[l + 1]]
            out_shape = (jax.ShapeDtypeStruct((t, d), F32), jax.ShapeDtypeStruct((t, d), BF16))
            out_specs = (row_blk, row_blk)
        else:
            out_shape = (jax.ShapeDtypeStruct((tp, d), F32), jax.ShapeDtypeStruct((ts, d), F32))
            out_specs = (pl.BlockSpec((TM_CMB, d), lambda m: (jnp.minimum(m, npb_c - 1), 0)),
                         pl.BlockSpec((TM_CMB, d), lambda m: (jnp.maximum(m - npb_c, 0), 0)))
        res = pl.pallas_call(
            functools.partial(_combine_kernel, with_next, npb_c),
            out_shape=out_shape, grid=(n_cmb,), in_specs=in_specs, out_specs=out_specs,
            scratch_shapes=[pltpu.VMEM((2, TOP_K * TM_CMB, d), F32), pltpu.SemaphoreType.DMA((2,))],
            compiler_params=_cp(1),
            name="moe_combine",
        )(*args)
        if with_next:
            x, h = res

    y_prompt, y_sample = res
    return (y_prompt.reshape(batch, seq, d), y_sample.reshape(dec_batch, dec_seq, d),
            jnp.stack(a_k, axis=1), jnp.stack(a_v, axis=1), jnp.stack(c_kv, axis=1),
            jnp.stack(c_pe, axis=1), jnp.stack(d_k, axis=1), jnp.stack(d_v, axis=1))
```

```python
import functools

import jax
import jax.numpy as jnp
import numpy as np
from jax import lax
from jax.experimental import pallas as pl
from jax.experimental.pallas import tpu as pltpu

F32 = jnp.float32
BF16 = jnp.bfloat16

HEAD = 128
LANES = 128
EPS = 1e-6
ROPE_THETA = 10000.0
GRID_W = 64
NEG_INF = -1e30
N_MOD = 6
TOP_K = 2
N_GROUPS = 4
EXPERTS_PER_GROUP = 4
N_EXPERTS = N_GROUPS * EXPERTS_PER_GROUP
WINDOW = 128
VMEM_LIMIT = 56 * 1024 * 1024

TM = 1024
TM_OUT = 512
TN_OUT = 512
TM_CMB = 256
MLA_HEADS_PER_STEP = ((1, 1), (4, 2))
EXPERT_TILING = ((256, 1), (512, 1), (256, 2), (128, 4))
TN_EVEN = 768
TN_ODD = 1024
MOD_TN = 1024
MOD_ROWS = 16


def _cp(n_axes, vmem=VMEM_LIMIT):
    return pltpu.CompilerParams(dimension_semantics=("arbitrary",) * n_axes,
                                vmem_limit_bytes=vmem)


def _silu(x):
    return x * (1.0 / (1.0 + jnp.exp(-x)))


def _rms_scale(sumsq, n):
    return lax.rsqrt(sumsq * (1.0 / n) + EPS)


def _rope(y, cos, sin, q):
    lane = lax.broadcasted_iota(jnp.int32, y.shape, 1)
    first = jnp.bitwise_and(lane, 2 * q - 1) < q
    sw = jnp.where(first, pltpu.roll(y, LANES - q, 1), pltpu.roll(y, q, 1))
    return y * cos + sw * sin


def _rope_tables(n_tok, rot_dim):
    rows = n_tok // GRID_W
    row_id = jnp.repeat(jnp.arange(rows, dtype=F32), GRID_W)
    col_id = jnp.tile(jnp.arange(GRID_W, dtype=F32), rows)
    axis_dim = rot_dim // 2
    inv_freq = ROPE_THETA ** (-jnp.arange(0, axis_dim, 2, dtype=F32) / axis_dim)
    ang_r = row_id[:, None] * inv_freq[None, :]
    ang_c = col_id[:, None] * inv_freq[None, :]
    cr, sr, cc, sc = jnp.cos(ang_r), jnp.sin(ang_r), jnp.cos(ang_c), jnp.sin(ang_c)
    cos = jnp.concatenate([cr, cr, cc, cc], axis=-1)
    sin = jnp.concatenate([-sr, sr, -sc, sc], axis=-1)
    pad = LANES - rot_dim
    if pad:
        cos = jnp.pad(cos, ((0, 0), (0, pad)))
        sin = jnp.pad(sin, ((0, 0), (0, pad)))
    return cos, sin


def _mod_kernel(c_ref, w_ref, b_ref, o_ref):
    s = _silu(c_ref[...]).astype(BF16)
    o_ref[...] = jnp.dot(s, w_ref[...].astype(BF16), preferred_element_type=F32) + b_ref[...]


def _modulation(cvec, w_mod, b_mod):
    depth, d, n = w_mod.shape
    return pl.pallas_call(
        _mod_kernel,
        out_shape=jax.ShapeDtypeStruct((depth, MOD_ROWS, n), F32),
        grid=(depth, n // MOD_TN),
        in_specs=[pl.BlockSpec((MOD_ROWS, d), lambda l, j: (0, 0)),
                  pl.BlockSpec((None, d, MOD_TN), lambda l, j: (l, 0, j)),
                  pl.BlockSpec((None, 1, MOD_TN), lambda l, j: (l, 0, j))],
        out_specs=pl.BlockSpec((None, MOD_ROWS, MOD_TN), lambda l, j: (l, 0, j)),
        compiler_params=_cp(2),
        name="modulation",
    )(cvec, w_mod, b_mod.reshape(depth, 1, n))


def _adaln(x, g, shift, scale):
    ss = jnp.sum(x * x, axis=-1, keepdims=True)
    y = (x * _rms_scale(ss, x.shape[-1])) * g
    return y * (1.0 + scale) + shift


def _pre_kernel(npb, xp_ref, xs_ref, g_ref, sh_ref, sc_ref, x_ref, h_ref):
    def emit(src_ref):
        x = src_ref[...]
        x_ref[...] = x
        h_ref[...] = _adaln(x, g_ref[...], sh_ref[0], sc_ref[0]).astype(BF16)

    @pl.when(pl.program_id(0) < npb)
    def _():
        emit(xp_ref)

    @pl.when(pl.program_id(0) >= npb)
    def _():
        emit(xs_ref)


def _conv_gate(u, gb, gc, wc_ref, seq):
    z = gc * u
    rows = z.shape[0]
    pos = jnp.bitwise_and(lax.broadcasted_iota(jnp.int32, z.shape, 0), seq - 1)
    zp = jnp.where(pos == 0, 0.0, pltpu.roll(z, 1, 0))
    zn = jnp.where(pos == seq - 1, 0.0, pltpu.roll(z, rows - 1, 0))
    w = wc_ref[...]
    y = zp * w[0:1, :] + z * w[1:2, :] + zn * w[2:3, :]
    return gb * y


def _in_even_kernel(npb, seq_p, seq_s, n_qkv, n_normed, n_kvf0,
                    h_ref, wa_ref, wb_ref, wc3_ref, gcol_ref, cos_ref, sin_ref, wc_ref,
                    qkv_ref, kvf_ref, b_ref):
    m = pl.program_id(0)
    n = pl.program_id(1)
    h = h_ref[...]
    accs = [jnp.dot(h, w[...], preferred_element_type=F32) for w in (wa_ref, wb_ref, wc3_ref)]
    piece = accs[0].shape[1]
    per = piece // HEAD
    heads = 3 * per

    def qkv_epilogue(sample):
        ys = []
        for j in range(heads):
            sl = slice(j * HEAD, (j + 1) * HEAD)
            a = accs[j // per][:, (j % per) * HEAD:(j % per + 1) * HEAD]
            normed = n * heads + j < n_normed
            ss = jnp.sum(a * a, axis=-1, keepdims=True)
            y = (a * _rms_scale(ss, HEAD)) * gcol_ref[:, sl]
            if sample:
                y = _rope(y, cos_ref[...], sin_ref[...], HEAD // 4)
            y = jnp.where(normed, y, a)
            qkv_ref[:, sl] = y.astype(BF16)
            ys.append(y)

        @pl.when(n >= n_kvf0)
        def _():
            for j in range(heads):
                kvf_ref[:, j * HEAD:(j + 1) * HEAD] = ys[j]

    @pl.when(jnp.logical_and(n < n_qkv, m < npb))
    def _():
        qkv_epilogue(False)

    @pl.when(jnp.logical_and(n < n_qkv, m >= npb))
    def _():
        qkv_epilogue(True)

    @pl.when(jnp.logical_and(n >= n_qkv, m < npb))
    def _():
        b_ref[...] = _conv_gate(accs[0], accs[1], accs[2], wc_ref, seq_p).astype(BF16)

    @pl.when(jnp.logical_and(n >= n_qkv, m >= npb))
    def _():
        b_ref[...] = _conv_gate(accs[0], accs[1], accs[2], wc_ref, seq_s).astype(BF16)


def _in_odd_kernel(npb, kv_lora,
                   h_ref, w_ref, gcol_ref, cos_ref, sin_ref,
                   cqn_ref, o1b_ref, o1f_ref, qd_ref, o4b_ref, o4f_ref):
    m = pl.program_id(0)
    n = pl.program_id(1)
    acc = jnp.dot(h_ref[...], w_ref[...], preferred_element_type=F32)
    width = acc.shape[1]

    def head_norm(a, sl, sample):
        ss = jnp.sum(a * a, axis=-1, keepdims=True)
        y = (a * _rms_scale(ss, HEAD)) * gcol_ref[:, sl]
        if sample:
            y = _rope(y, cos_ref[...], sin_ref[...], HEAD // 4)
        return y

    @pl.when(n == 0)
    def _():
        ss = jnp.sum(acc * acc, axis=-1, keepdims=True)
        cqn_ref[...] = ((acc * _rms_scale(ss, width)) * gcol_ref[...]).astype(BF16)

    def block1(sample):
        a = acc[:, :kv_lora]
        ss = jnp.sum(a * a, axis=-1, keepdims=True)
        y = (a * _rms_scale(ss, kv_lora)) * gcol_ref[:, :kv_lora]
        o1f_ref[:, :kv_lora] = y
        o1b_ref[:, :kv_lora] = y.astype(BF16)
        for j in range(kv_lora // HEAD, width // HEAD):
            sl = slice(j * HEAD, (j + 1) * HEAD)
            y = head_norm(acc[:, sl], sl, sample)
            o1f_ref[:, sl] = y
            o1b_ref[:, sl] = y.astype(BF16)

    def blockq(sample):
        for j in range(width // HEAD):
            sl = slice(j * HEAD, (j + 1) * HEAD)
            qd_ref[:, sl] = head_norm(acc[:, sl], sl, sample).astype(BF16)

    @pl.when(jnp.logical_and(n == 1, m < npb))
    def _():
        block1(False)

    @pl.when(jnp.logical_and(n == 1, m >= npb))
    def _():
        block1(True)

    is_q = jnp.logical_or(n == 2, n == 3)

    @pl.when(jnp.logical_and(is_q, m < npb))
    def _():
        blockq(False)

    @pl.when(jnp.logical_and(is_q, m >= npb))
    def _():
        blockq(True)

    @pl.when(n == 4)
    def _():
        o4f_ref[...] = acc
        o4b_ref[...] = acc.astype(BF16)


def _qb_kernel(npb, n_heads, qk_head,
               x_ref, w_ref, g_ref, cos_ref, sin_ref, q_ref):
    m = pl.program_id(0)
    acc = jnp.dot(x_ref[...], w_ref[...], preferred_element_type=F32)

    def body(sample):
        for h in range(n_heads):
            s0 = slice(2 * h * HEAD, (2 * h + 1) * HEAD)
            s1 = slice((2 * h + 1) * HEAD, (2 * h + 2) * HEAD)
            nope, pe = acc[:, s0], acc[:, s1]
            ss = jnp.sum(nope * nope + pe * pe, axis=-1, keepdims=True)
            r = _rms_scale(ss, qk_head)
            q_ref[:, s0] = ((nope * r) * g_ref[:, s0]).astype(BF16)
            y = (pe * r) * g_ref[:, s1]
            if sample:
                y = _rope(y, cos_ref[...], sin_ref[...], HEAD // 8)
            q_ref[:, s1] = y.astype(BF16)

    @pl.when(m < npb)
    def _():
        body(False)

    @pl.when(m >= npb)
    def _():
        body(True)


def _kvb_kernel(lo_s, hi_s, n_heads, qk_head,
                x_ref, pe_ref, w_ref, g_ref, cos_ref, sin_ref, k_ref, v_ref):
    m = pl.program_id(0)
    acc = jnp.dot(x_ref[...], w_ref[...], preferred_element_type=F32)
    kpe = pe_ref[...]
    ss_pe = jnp.sum(kpe * kpe, axis=-1, keepdims=True)
    nk = n_heads * HEAD

    def body(sample):
        pe_g = kpe * g_ref[:, HEAD:2 * HEAD]
        if sample:
            pe_g = _rope(pe_g, cos_ref[...], sin_ref[...], HEAD // 8)
        for h in range(n_heads):
            kn = acc[:, h * HEAD:(h + 1) * HEAD]
            ss = jnp.sum(kn * kn, axis=-1, keepdims=True) + ss_pe
            r = _rms_scale(ss, qk_head)
            k_ref[:, 2 * h * HEAD:(2 * h + 1) * HEAD] = ((kn * r) * g_ref[:, :HEAD]).astype(BF16)
            k_ref[:, (2 * h + 1) * HEAD:(2 * h + 2) * HEAD] = (pe_g * r).astype(BF16)
            v_ref[:, h * HEAD:(h + 1) * HEAD] = acc[:, nk + h * HEAD:nk + (h + 1) * HEAD].astype(BF16)

    is_s = jnp.logical_and(m >= lo_s, m < hi_s)

    @pl.when(jnp.logical_not(is_s))
    def _():
        body(False)

    @pl.when(is_s)
    def _():
        body(True)


def _softmax_pv(s, v, sink=None):
    mx = jnp.max(s, axis=-1, keepdims=True)
    if sink is not None:
        mx = jnp.maximum(mx, sink)
    p = jnp.exp(s - mx)
    l = jnp.sum(p, axis=-1, keepdims=True)
    if sink is not None:
        l = l + jnp.exp(sink - mx)
    p = p * (1.0 / l)
    return jnp.dot(p.astype(BF16), v, preferred_element_type=F32)


def _qk(q, k):
    return lax.dot_general(q, k, (((1,), (1,)), ((), ())), preferred_element_type=F32)


def _attn_prompt_kernel(n_seq, seq, n_kv, groups, dk, scale, use_sink, *refs):
    if use_sink:
        sink_ref, q_ref, k_ref, v_ref, o_ref = refs
    else:
        q_ref, k_ref, v_ref, o_ref = refs
    j = pl.program_id(1)
    dv = v_ref.shape[1] // n_kv
    for b in range(n_seq):
        rows = slice(b * seq, (b + 1) * seq)
        for c in range(n_kv):
            k = k_ref[rows, c * dk:(c + 1) * dk]
            v = v_ref[rows, c * dv:(c + 1) * dv]
            for g in range(groups):
                hq = c * groups + g
                q = q_ref[rows, hq * dk:(hq + 1) * dk]
                s = _qk(q, k) * scale
                sink = sink_ref[(j * n_kv + c) * groups + g] if use_sink else None
                o_ref[rows, hq * dv:(hq + 1) * dv] = _softmax_pv(s, v, sink).astype(o_ref.dtype)


def _attn_sample_kernel(seq, qblk, n_kv, groups, dk, scale, *refs):
    q_ref, k_ref, v_ref, ck_ref, cv_ref, a_ref, o_ref = refs
    del a_ref
    dv = v_ref.shape[1] // n_kv
    ks = [jnp.concatenate([k_ref[:, c * dk:(c + 1) * dk],
                           ck_ref[:, c * dk:(c + 1) * dk].astype(BF16)], axis=0) for c in range(n_kv)]
    vs = [jnp.concatenate([v_ref[:, c * dv:(c + 1) * dv],
                           cv_ref[:, c * dv:(c + 1) * dv].astype(BF16)], axis=0) for c in range(n_kv)]

    def step(i, carry):
        r0 = pl.multiple_of(i * qblk, qblk)
        for c in range(n_kv):
            for g in range(groups):
                hq = c * groups + g
                q = q_ref[pl.ds(r0, qblk), hq * dk:(hq + 1) * dk]
                s = _qk(q, ks[c]) * scale
                o_ref[pl.ds(r0, qblk), hq * dv:(hq + 1) * dv] = (
                    _softmax_pv(s, vs[c]).astype(o_ref.dtype))
        return carry

    lax.fori_loop(0, seq // qblk, step, 0)


def _attn_window_kernel(seq, groups, scale, sink_ref, q_ref, k_ref, v_ref, ck_ref, cv_ref,
                        a_ref, o_ref):
    del a_ref
    j = pl.program_id(1)
    ck = ck_ref[...].astype(BF16)
    cv = cv_ref[...].astype(BF16)
    w = WINDOW
    rid = lax.shift_right_logical(lax.broadcasted_iota(jnp.int32, (groups * w, 1), 0),
                                  int(np.log2(w)))
    sink = jnp.zeros((groups * w, 1), F32)
    for g in range(groups):
        sink = jnp.where(rid == g, sink_ref[j * groups + g], sink)
    for qb in range(seq // w):
        lo = max(0, (qb - 1) * w)
        hi = min(seq, (qb + 2) * w)
        rows = slice(qb * w, (qb + 1) * w)
        q = jnp.concatenate([q_ref[rows, g * HEAD:(g + 1) * HEAD] for g in range(groups)], axis=0)
        kl = k_ref[lo:hi, :]
        vl = v_ref[lo:hi, :]
        s_loc = _qk(q, kl) * scale
        qpos = qb * w + jnp.bitwise_and(lax.broadcasted_iota(jnp.int32, s_loc.shape, 0), w - 1)
        kpos = lo + lax.broadcasted_iota(jnp.int32, s_loc.shape, 1)
        s_loc = jnp.where(jnp.abs(qpos - kpos) <= w, s_loc, NEG_INF)
        s_ctx = _qk(q, ck) * scale
        mx = jnp.maximum(jnp.maximum(jnp.max(s_loc, axis=-1, keepdims=True),
                                     jnp.max(s_ctx, axis=-1, keepdims=True)), sink)
        p_loc = jnp.exp(s_loc - mx)
        p_ctx = jnp.exp(s_ctx - mx)
        l = (jnp.sum(p_loc, axis=-1, keepdims=True) + jnp.sum(p_ctx, axis=-1, keepdims=True)
             + jnp.exp(sink - mx))
        inv = 1.0 / l
        o = (jnp.dot((p_loc * inv).astype(BF16), vl, preferred_element_type=F32)
             + jnp.dot((p_ctx * inv).astype(BF16), cv, preferred_element_type=F32))
        for g in range(groups):
            o_ref[rows, g * HEAD:(g + 1) * HEAD] = o[g * w:(g + 1) * w, :].astype(o_ref.dtype)


def _out_kernel(n_chunks, a_ref, b_ref, w_ref, x_ref, gate_ref, g2_ref, sh_ref, sc_ref,
                wr_ref, rb_ref, xo_ref, h2_ref, lg_ref, xs_ref):
    n = pl.program_id(1)
    ka = a_ref.shape[1]
    acc = (jnp.dot(a_ref[...], w_ref[:ka, :], preferred_element_type=F32)
           + jnp.dot(b_ref[...], w_ref[ka:, :], preferred_element_type=F32))
    xn = x_ref[...] + gate_ref[0] * acc
    xo_ref[...] = xn
    xs_ref[n] = xn

    @pl.when(n == n_chunks - 1)
    def _():
        tn = xn.shape[1]
        ss = jnp.zeros((xn.shape[0], 1), F32)
        for c in range(n_chunks):
            xc = xs_ref[c]
            ss = ss + jnp.sum(xc * xc, axis=-1, keepdims=True)
        r = _rms_scale(ss, n_chunks * tn)
        lg = jnp.zeros(lg_ref.shape, F32) + rb_ref[...]
        for c in range(n_chunks):
            sl = slice(c * tn, (c + 1) * tn)
            y = (xs_ref[c] * r) * g2_ref[:, sl]
            h = y * (1.0 + sc_ref[0][:, sl]) + sh_ref[0][:, sl]
            hi = h.astype(BF16)
            lo = (h - hi.astype(F32)).astype(BF16)
            h2_ref[:, sl] = hi
            both = jnp.dot(hi, wr_ref[sl, :], preferred_element_type=F32)
            lg = lg + (both[:, :LANES] + both[:, LANES:]
                       + jnp.dot(lo, wr_ref[sl, :LANES], preferred_element_type=F32))
        lg_ref[...] = lg


def _group_pipeline(nblk, gsz, in_copies, out_copy, compute):
    shift = int(np.log2(gsz))
    ngrp = lax.shift_right_logical(nblk + (gsz - 1), shift)

    def count(g):
        return jnp.minimum(gsz, nblk - g * gsz)

    def each_in(g, slot, fn):
        c = count(g)
        for k in range(gsz):
            @pl.when(k < c)
            def _():
                for cp in in_copies(g * gsz + k, slot, k):
                    fn(cp)

    def by_count(c, fn):
        for s in range(1, gsz + 1):
            @pl.when(c == s)
            def _():
                fn(s)

    @pl.when(nblk > 0)
    def _():
        each_in(0, 0, lambda cp: cp.start())

        def body(g, carry):
            slot = jnp.bitwise_and(g, 1)

            @pl.when(g + 1 < ngrp)
            def _():
                each_in(g + 1, 1 - slot, lambda cp: cp.start())

            each_in(g, slot, lambda cp: cp.wait())

            @pl.when(g >= 2)
            def _():
                out_copy(g - 2, slot, gsz).wait()

            def run(s):
                compute(slot, s)
                out_copy(g, slot, s).start()

            by_count(count(g), run)
            return carry

        lax.fori_loop(0, ngrp, body, 0)

        @pl.when(ngrp >= 2)
        def _():
            out_copy(ngrp - 2, jnp.bitwise_and(ngrp, 1), gsz).wait()

        last = ngrp - 1
        by_count(count(last), lambda s: out_copy(last, jnp.bitwise_and(last, 1), s).wait())


def _up_kernel(tb, gsz, b0_ref, nb_ref, x_hbm, wg_ref, wu_ref, a_hbm,
               wg_s, wu_s, xbuf, obuf, sem_in, sem_out):
    cc = pl.program_id(0)
    e = pl.program_id(1)
    first = b0_ref[e]

    def rows(blk, n):
        return pl.ds(pl.multiple_of((first + blk) * tb, tb), n * tb)

    def in_copies(blk, slot, k):
        return [pltpu.make_async_copy(x_hbm.at[rows(blk, 1), :],
                                      xbuf.at[slot, k * tb:(k + 1) * tb], sem_in.at[slot])]

    def out_copy(g, slot, c):
        return pltpu.make_async_copy(obuf.at[slot, :c * tb], a_hbm.at[cc, rows(g * gsz, c), :],
                                     sem_out.at[slot])

    def compute(slot, c):
        x = xbuf[slot, :c * tb]
        g = jnp.dot(x, wg_s[...], preferred_element_type=F32)
        u = jnp.dot(x, wu_s[...], preferred_element_type=F32)
        obuf[slot, :c * tb] = (_silu(g) * u).astype(BF16)

    wg_s[...] = wg_ref[...].astype(BF16)
    wu_s[...] = wu_ref[...].astype(BF16)
    _group_pipeline(nb_ref[e], gsz, in_copies, out_copy, compute)


def _down_kernel(tb, gsz, b0_ref, nb_ref, a_hbm, wd_ref, y_hbm, wd_s, abuf, ybuf, sem_in, sem_out):
    e = pl.program_id(0)
    half = pl.program_id(1)
    first = b0_ref[e]
    n_half, _, fc = a_hbm.shape
    dn = wd_s.shape[1]

    def rows(blk, n):
        return pl.ds(pl.multiple_of((first + blk) * tb, tb), n * tb)

    def in_copies(blk, slot, k):
        return [pltpu.make_async_copy(a_hbm.at[j, rows(blk, 1), :],
                                      abuf.at[slot, k * tb:(k + 1) * tb, j * fc:(j + 1) * fc],
                                      sem_in.at[slot, j])
                for j in range(n_half)]

    def out_copy(g, slot, c):
        cols = pl.ds(pl.multiple_of(half * dn, dn), dn)
        return pltpu.make_async_copy(ybuf.at[slot, :c * tb], y_hbm.at[rows(g * gsz, c), cols],
                                     sem_out.at[slot])

    def compute(slot, c):
        ybuf[slot, :c * tb] = jnp.dot(abuf[slot, :c * tb], wd_s[...], preferred_element_type=F32)

    wd_s[...] = wd_ref[...].astype(BF16)
    _group_pipeline(nb_ref[e], gsz, in_copies, out_copy, compute)


def _combine_kernel(with_next, npb, tbl_ref, tbl_next_ref, x_ref, y_hbm, gw_ref, gate_ref, *refs):
    if with_next:
        g_ref, sh_ref, sc_ref, xo_ref, h_ref, ybuf, sem = refs
    else:
        xp_ref, xs_ref, ybuf, sem = refs
    m = pl.program_id(0)
    n_steps = pl.num_programs(0)
    n_rows = ybuf.shape[1]
    tm = n_rows // TOP_K
    slot = jnp.bitwise_and(m, 1)

    def row_copy(tbl, r, s):
        return pltpu.make_async_copy(y_hbm.at[pl.ds(tbl[0, r], 1), :], ybuf.at[s, pl.ds(r, 1), :],
                                     sem.at[s])

    def start_all(tbl, s):
        def body(r, carry):
            row_copy(tbl, r, s).start()
            return carry
        lax.fori_loop(0, n_rows, body, 0, unroll=8)

    def wait_all(tbl, s):
        def body(r, carry):
            row_copy(tbl, r, s).wait()
            return carry
        lax.fori_loop(0, n_rows, body, 0, unroll=8)

    @pl.when(m == 0)
    def _():
        start_all(tbl_ref, 0)

    @pl.when(m + 1 < n_steps)
    def _():
        start_all(tbl_next_ref, 1 - slot)

    wait_all(tbl_ref, slot)
    gw = gw_ref[...]
    yb = ybuf.at[slot]
    moe = gw[:, 0:1] * yb[:tm, :] + gw[:, 1:2] * yb[tm:, :]
    xn = x_ref[...] + gate_ref[0] * moe
    if with_next:
        xo_ref[...] = xn
        h_ref[...] = _adaln(xn, g_ref[...], sh_ref[0], sc_ref[0]).astype(BF16)
    else:
        @pl.when(m < npb)
        def _():
            xp_ref[...] = xn

        @pl.when(m >= npb)
        def _():
            xs_ref[...] = xn


def kernel(x_prompt, x_sample, cache_a_k, cache_a_v, cache_c_kv, cache_c_pe, cache_d_k, cache_d_v,
           c, c_ctx, w_mod, b_mod, norm1_g, norm2_g, w_in_even, w_out_even, a_q_norm, a_k_norm,
           b_conv, w_in_odd, w_out_odd, c_q_a_norm, c_kv_a_norm, c_w_q_b, c_w_kv_b, c_q_norm,
           c_k_norm, d_q_norm, d_k_norm, d_sink, w_group_router, b_group_router, w_expert_router,
           b_expert_router, w_expert_gate, w_expert_up, w_expert_down):
    batch, seq, d = x_prompt.shape
    dec_batch, dec_seq, _ = x_sample.shape
    past = cache_a_k.shape[2]
    depth = w_mod.shape[0]
    kvh = cache_a_k.shape[3]
    b_width = d // 2
    a_width = w_out_even.shape[1] - b_width
    a_heads = a_width // HEAD
    groups = a_heads // kvh
    kv_width = kvh * HEAD
    q_lora = c_q_a_norm.shape[1]
    kv_lora = c_kv_a_norm.shape[1]
    qk_head = c_q_norm.shape[1]
    qk_rope = cache_c_pe.shape[3]
    qk_nope = qk_head - qk_rope
    c_heads = c_w_q_b.shape[2] // qk_head
    v_dim = c_w_kv_b.shape[2] // c_heads - qk_nope
    d_width = d_sink.shape[1] * HEAD
    d_expert = w_expert_gate.shape[3]
    tp = batch * seq
    ts = dec_batch * dec_seq
    t = tp + ts
    assert qk_nope == HEAD and v_dim == HEAD and 2 * qk_rope == HEAD and d_width == a_width
    assert seq == past and dec_seq == TM and tp % TM == 0 and TM % seq == 0

    def row_of(tm):
        npb, bps = tp // tm, dec_seq // tm
        return lambda m: jnp.where(m < npb, 0, 1 + (m - npb) // bps)

    cvec = jnp.zeros((MOD_ROWS, d), F32).at[0].set(c_ctx).at[1:1 + dec_batch].set(c)
    mods = _modulation(cvec, w_mod, b_mod)
    n_rows = 1 + dec_batch
    modtab = (mods[:, :n_rows].reshape(depth, n_rows, N_MOD, d).transpose(0, 2, 1, 3)
              .reshape(depth, N_MOD * n_rows, 1, d))

    def mod_spec(j, tm, cols=None, col_axis=None):
        r = row_of(tm)
        if cols is None:
            return pl.BlockSpec((1, 1, d), lambda *g: (j * n_rows + r(g[0]), 0, 0))
        return pl.BlockSpec((1, 1, cols), lambda *g: (j * n_rows + r(g[0]), 0, g[col_axis]))

    cos_hd, sin_hd = _rope_tables(dec_seq, HEAD)
    cos_pe, sin_pe = _rope_tables(dec_seq, qk_rope)

    npb = tp // TM
    npb_o = tp // TM_CMB

    full = lambda shape: pl.BlockSpec(shape, lambda *g: (0,) * len(shape))

    x, h = pl.pallas_call(
        functools.partial(_pre_kernel, npb_o),
        out_shape=(jax.ShapeDtypeStruct((t, d), F32), jax.ShapeDtypeStruct((t, d), BF16)),
        grid=(t // TM_CMB,),
        in_specs=[pl.BlockSpec((TM_CMB, d), lambda m: (jnp.minimum(m, npb_o - 1), 0)),
                  pl.BlockSpec((TM_CMB, d), lambda m: (jnp.maximum(m - npb_o, 0), 0)),
                  full((1, d)), mod_spec(0, TM_CMB), mod_spec(1, TM_CMB)],
        out_specs=(pl.BlockSpec((TM_CMB, d), lambda m: (m, 0)),
                   pl.BlockSpec((TM_CMB, d), lambda m: (m, 0))),
        compiler_params=_cp(1),
        name="adaln_first",
    )(x_prompt.reshape(tp, d), x_sample.reshape(ts, d), norm1_g[0:1], modtab[0], modtab[0])

    a_k, a_v, c_kv, c_pe, d_k, d_v = [], [], [], [], [], []
    w_in_even16 = w_in_even.astype(BF16)
    w_out16 = (w_out_even.astype(BF16), w_out_odd.astype(BF16))

    for l in range(depth):
        i = l // 2
        mt = modtab[l]
        w_out = w_out16[l % 2]
        if l % 2 == 0:
            w_in = w_in_even16
            qkv_w = a_width + 2 * kv_width
            cch = TN_EVEN // 3
            n_qkv = qkv_w // TN_EVEN
            pb = qkv_w // cch
            nch = b_width // cch
            n_kvf0 = a_width // TN_EVEN
            kvf_off = a_width - n_kvf0 * TN_EVEN
            gcol = jnp.concatenate([jnp.tile(a_q_norm[i], a_heads), jnp.tile(a_k_norm[i], kvh),
                                    jnp.ones((kv_width,), F32)])[None, :]
            n_in = n_qkv + nch

            def piece(k):
                return pl.BlockSpec((None, d, cch), lambda m, n: (
                    i, 0, jnp.where(n < n_qkv, 3 * n + k, pb + k * nch + n - n_qkv)))

            qkv, kvf, bconv = pl.pallas_call(
                functools.partial(_in_even_kernel, npb, seq, dec_seq, n_qkv, a_heads + kvh, n_kvf0),
                out_shape=(jax.ShapeDtypeStruct((t, qkv_w), BF16),
                           jax.ShapeDtypeStruct((t, (n_qkv - n_kvf0) * TN_EVEN), F32),
                           jax.ShapeDtypeStruct((t, b_width), BF16)),
                grid=(t // TM, n_in),
                in_specs=[pl.BlockSpec((TM, d), lambda m, n: (m, 0)),
                          piece(0), piece(1), piece(2),
                          pl.BlockSpec((1, TN_EVEN), lambda m, n: (0, jnp.minimum(n, n_qkv - 1))),
                          full((TM, LANES)), full((TM, LANES)),
                          pl.BlockSpec((3, cch), lambda m, n: (0, jnp.maximum(n - n_qkv, 0)))],
                out_specs=(pl.BlockSpec((TM, TN_EVEN), lambda m, n: (m, jnp.minimum(n, n_qkv - 1))),
                           pl.BlockSpec((TM, TN_EVEN),
                                        lambda m, n: (m, jnp.clip(n - n_kvf0, 0, n_qkv - n_kvf0 - 1))),
                           pl.BlockSpec((TM, cch), lambda m, n: (m, jnp.maximum(n - n_qkv, 0)))),
                compiler_params=_cp(2),
                name="in_proj_even",
            )(h, w_in, w_in, w_in, gcol, cos_hd, sin_hd, b_conv[i])
            a_k.append(kvf[:tp, kvf_off:kvf_off + kv_width].reshape(batch, seq, kvh, HEAD))
            a_v.append(kvf[:tp, kvf_off + kv_width:kvf_off + 2 * kv_width]
                       .reshape(batch, seq, kvh, HEAD))

            gw_ = groups * HEAD
            qb0, kb, vb = 0, a_width // HEAD, (a_width + kv_width) // HEAD
            scale = HEAD ** -0.5
            att = pl.pallas_call(
                functools.partial(_attn_prompt_kernel, TM // seq, seq, 1, groups, HEAD, scale, False),
                out_shape=jax.ShapeDtypeStruct((t, a_width), BF16),
                grid=(npb, kvh),
                in_specs=[pl.BlockSpec((TM, gw_), lambda m, j: (m, qb0 + j)),
                          pl.BlockSpec((TM, HEAD), lambda m, j: (m, kb + j)),
                          pl.BlockSpec((TM, HEAD), lambda m, j: (m, vb + j))],
                out_specs=pl.BlockSpec((TM, gw_), lambda m, j: (m, j)),
                compiler_params=_cp(2),
                name="attn_a_prompt",
            )(qkv, qkv, qkv)
            ck = cache_a_k[:, i].reshape(dec_batch, past, kv_width)
            cv = cache_a_v[:, i].reshape(dec_batch, past, kv_width)
            att = pl.pallas_call(
                functools.partial(_attn_sample_kernel, dec_seq, 256, 1, groups, HEAD, scale),
                out_shape=jax.ShapeDtypeStruct((t, a_width), BF16),
                grid=(dec_batch, kvh),
                in_specs=[pl.BlockSpec((TM, gw_), lambda b, j: (npb + b, qb0 + j)),
                          pl.BlockSpec((TM, HEAD), lambda b, j: (npb + b, kb + j)),
                          pl.BlockSpec((TM, HEAD), lambda b, j: (npb + b, vb + j)),
                          pl.BlockSpec((None, past, HEAD), lambda b, j: (b, 0, j)),
                          pl.BlockSpec((None, past, HEAD), lambda b, j: (b, 0, j)),
                          pl.BlockSpec(memory_space=pl.ANY)],
                out_specs=pl.BlockSpec((TM, gw_), lambda b, j: (npb + b, j)),
                input_output_aliases={5: 0},
                compiler_params=_cp(2),
                name="attn_a_sample",
            )(qkv, qkv, qkv, ck, cv, att)
            lhs_a, lhs_b = att, bconv
        else:
            w = w_in_odd[i]
            o = np.cumsum([0, q_lora, kv_lora, qk_rope, d_width, kv_width, kv_width])
            cq_w, ckv_w, kpe_w, dq_w, dk_w, dv_w = [w[:, o[s]:o[s + 1]] for s in range(6)]
            tail = TN_ODD - kv_width - qk_rope
            w_in = jnp.concatenate([cq_w, ckv_w, dk_w, dq_w, dv_w, kpe_w,
                                    jnp.zeros((d, tail), F32)], axis=1).astype(BF16)
            assert q_lora == TN_ODD and kv_lora + kv_width == TN_ODD and d_width == 2 * TN_ODD
            gcol = jnp.concatenate([c_q_a_norm[i], c_kv_a_norm[i], jnp.tile(d_k_norm[i], kvh),
                                    jnp.tile(d_q_norm[i], d_width // HEAD),
                                    jnp.ones((TN_ODD,), F32)])[None, :]
            n_in = w_in.shape[1] // TN_ODD
            blk = lambda fn: pl.BlockSpec((TM_OUT, TN_ODD), fn)
            spb = dec_seq // TM_OUT
            cqn, o1b, o1f, qd, o4b, o4f = pl.pallas_call(
                functools.partial(_in_odd_kernel, tp // TM_OUT, kv_lora),
                out_shape=(jax.ShapeDtypeStruct((t, TN_ODD), BF16),
                           jax.ShapeDtypeStruct((t, TN_ODD), BF16),
                           jax.ShapeDtypeStruct((t, TN_ODD), F32),
                           jax.ShapeDtypeStruct((t, d_width), BF16),
                           jax.ShapeDtypeStruct((t, TN_ODD), BF16),
                           jax.ShapeDtypeStruct((t, TN_ODD), F32)),
                grid=(t // TM_OUT, n_in),
                in_specs=[pl.BlockSpec((TM_OUT, d), lambda m, n: (m, 0)),
                          pl.BlockSpec((d, TN_ODD), lambda m, n: (0, n)),
                          pl.BlockSpec((1, TN_ODD), lambda m, n: (0, n)),
                          pl.BlockSpec((TM_OUT, LANES), lambda m, n: (m % spb, 0)),
                          pl.BlockSpec((TM_OUT, LANES), lambda m, n: (m % spb, 0))],
                out_specs=(blk(lambda m, n: (m, 0)), blk(lambda m, n: (m, 0)),
                           blk(lambda m, n: (m, 0)),
                           blk(lambda m, n: (m, jnp.clip(n - 2, 0, 1))),
                           blk(lambda m, n: (m, 0)), blk(lambda m, n: (m, 0))),
                compiler_params=_cp(2),
                name="in_proj_odd",
            )(h, w_in, gcol, cos_hd, sin_hd)
            c_kv.append(o1f[:tp, :kv_lora].reshape(batch, seq, kv_lora))
            d_k.append(o1f[:tp, kv_lora:].reshape(batch, seq, kvh, HEAD))
            d_v.append(o4f[:tp, :kv_width].reshape(batch, seq, kvh, HEAD))
            c_pe.append(o4f[:tp, kv_width:kv_width + qk_rope].reshape(batch, seq, qk_rope))

            wq = c_w_q_b[i].reshape(q_lora, c_heads, qk_head)
            wq = jnp.pad(wq, ((0, 0), (0, 0), (0, 2 * HEAD - qk_head)))
            wq = wq.reshape(q_lora, c_heads * 2 * HEAD).astype(BF16)
            gq = jnp.tile(jnp.pad(c_q_norm[i], (0, 2 * HEAD - qk_head)), c_heads)[None, :]
            cw = c_heads * 2 * HEAD
            q_c = pl.pallas_call(
                functools.partial(_qb_kernel, tp // TM_OUT, c_heads, qk_head),
                out_shape=jax.ShapeDtypeStruct((t, cw), BF16),
                grid=(t // TM_OUT,),
                in_specs=[pl.BlockSpec((TM_OUT, q_lora), lambda m: (m, 0)),
                          full((q_lora, cw)), full((1, cw)),
                          pl.BlockSpec((TM_OUT, LANES), lambda m: (m % (dec_seq // TM_OUT), 0)),
                          pl.BlockSpec((TM_OUT, LANES), lambda m: (m % (dec_seq // TM_OUT), 0))],
                out_specs=pl.BlockSpec((TM_OUT, cw), lambda m: (m, 0)),
                compiler_params=_cp(1),
                name="mla_q_up",
            )(cqn, wq, gq, cos_pe, sin_pe)

            wkv = c_w_kv_b[i].reshape(kv_lora, c_heads, qk_nope + v_dim)
            wkv = jnp.concatenate([wkv[:, :, :qk_nope].reshape(kv_lora, c_heads * qk_nope),
                                   wkv[:, :, qk_nope:].reshape(kv_lora, c_heads * v_dim)],
                                  axis=1).astype(BF16)
            n_ctx = dec_batch * past
            ckv_all = jnp.concatenate([o1b[:, :kv_lora],
                                       cache_c_kv[:, i].reshape(n_ctx, kv_lora).astype(BF16)], axis=0)
            kpe_all = jnp.concatenate(
                [o4f[:, kv_width:kv_width + HEAD],
                 jnp.pad(cache_c_pe[:, i].reshape(n_ctx, qk_rope), ((0, 0), (0, HEAD - qk_rope)))],
                axis=0)
            gk = jnp.pad(c_k_norm[i], (0, 2 * HEAD - qk_head))[None, :]
            t2 = t + n_ctx
            spb = dec_seq // TM_OUT
            k_c, v_c = pl.pallas_call(
                functools.partial(_kvb_kernel, tp // TM_OUT, t // TM_OUT, c_heads, qk_head),
                out_shape=(jax.ShapeDtypeStruct((t2, cw), BF16),
                           jax.ShapeDtypeStruct((t2, c_heads * v_dim), BF16)),
                grid=(t2 // TM_OUT,),
                in_specs=[pl.BlockSpec((TM_OUT, kv_lora), lambda m: (m, 0)),
                          pl.BlockSpec((TM_OUT, HEAD), lambda m: (m, 0)),
                          full((kv_lora, c_heads * (qk_nope + v_dim))), full((1, 2 * HEAD)),
                          pl.BlockSpec((TM_OUT, LANES), lambda m: (m % spb, 0)),
                          pl.BlockSpec((TM_OUT, LANES), lambda m: (m % spb, 0))],
                out_specs=(pl.BlockSpec((TM_OUT, cw), lambda m: (m, 0)),
                           pl.BlockSpec((TM_OUT, c_heads * v_dim), lambda m: (m, 0))),
                compiler_params=_cp(1),
                name="mla_kv_up",
            )(ckv_all, kpe_all, wkv, gk, cos_pe, sin_pe)

            scale_c = qk_head ** -0.5
            hp, hs = MLA_HEADS_PER_STEP[i % len(MLA_HEADS_PER_STEP)]
            oc = pl.pallas_call(
                functools.partial(_attn_prompt_kernel, TM // seq, seq, hp, 1, 2 * HEAD, scale_c,
                                  False),
                out_shape=jax.ShapeDtypeStruct((t, c_heads * v_dim), BF16),
                grid=(npb, c_heads // hp),
                in_specs=[pl.BlockSpec((TM, hp * 2 * HEAD), lambda m, j: (m, j)),
                          pl.BlockSpec((TM, hp * 2 * HEAD), lambda m, j: (m, j)),
                          pl.BlockSpec((TM, hp * v_dim), lambda m, j: (m, j))],
                out_specs=pl.BlockSpec((TM, hp * v_dim), lambda m, j: (m, j)),
                compiler_params=_cp(2),
                name="attn_c_prompt",
            )(q_c, k_c, v_c)
            cb0 = t // past
            oc = pl.pallas_call(
                functools.partial(_attn_sample_kernel, dec_seq, 256, hs, 1, 2 * HEAD, scale_c),
                out_shape=jax.ShapeDtypeStruct((t, c_heads * v_dim), BF16),
                grid=(dec_batch, c_heads // hs),
                in_specs=[pl.BlockSpec((TM, hs * 2 * HEAD), lambda b, j: (npb + b, j)),
                          pl.BlockSpec((TM, hs * 2 * HEAD), lambda b, j: (npb + b, j)),
                          pl.BlockSpec((TM, hs * v_dim), lambda b, j: (npb + b, j)),
                          pl.BlockSpec((past, hs * 2 * HEAD), lambda b, j: (cb0 + b, j)),
                          pl.BlockSpec((past, hs * v_dim), lambda b, j: (cb0 + b, j)),
                          pl.BlockSpec(memory_space=pl.ANY)],
                out_specs=pl.BlockSpec((TM, hs * v_dim), lambda b, j: (npb + b, j)),
                input_output_aliases={5: 0},
                compiler_params=_cp(2),
                name="attn_c_sample",
            )(q_c, k_c, v_c, k_c, v_c, oc)

            gw_ = groups * HEAD
            scale = HEAD ** -0.5
            sink = d_sink[i]
            kdb = kv_lora // HEAD
            od = pl.pallas_call(
                functools.partial(_attn_prompt_kernel, TM // seq, seq, 1, groups, HEAD, scale, True),
                out_shape=jax.ShapeDtypeStruct((t, d_width), BF16),
                grid_spec=pltpu.PrefetchScalarGridSpec(
                    num_scalar_prefetch=1, grid=(npb, kvh),
                    in_specs=[pl.BlockSpec((TM, gw_), lambda m, j, s: (m, j)),
                              pl.BlockSpec((TM, HEAD), lambda m, j, s: (m, kdb + j)),
                              pl.BlockSpec((TM, HEAD), lambda m, j, s: (m, j))],
                    out_specs=pl.BlockSpec((TM, gw_), lambda m, j, s: (m, j))),
                compiler_params=_cp(2),
                name="attn_d_prompt",
            )(sink, qd, o1b, o4b)
            ck = cache_d_k[:, i].reshape(dec_batch, past, kv_width)
            cv = cache_d_v[:, i].reshape(dec_batch, past, kv_width)
            od = pl.pallas_call(
                functools.partial(_attn_window_kernel, dec_seq, groups, scale),
                out_shape=jax.ShapeDtypeStruct((t, d_width), BF16),
                grid_spec=pltpu.PrefetchScalarGridSpec(
                    num_scalar_prefetch=1, grid=(dec_batch, kvh),
                    in_specs=[pl.BlockSpec((TM, gw_), lambda b, j, s: (npb + b, j)),
                              pl.BlockSpec((TM, HEAD), lambda b, j, s: (npb + b, kdb + j)),
                              pl.BlockSpec((TM, HEAD), lambda b, j, s: (npb + b, j)),
                              pl.BlockSpec((None, past, HEAD), lambda b, j, s: (b, 0, j)),
                              pl.BlockSpec((None, past, HEAD), lambda b, j, s: (b, 0, j)),
                              pl.BlockSpec(memory_space=pl.ANY)],
                    out_specs=pl.BlockSpec((TM, gw_), lambda b, j, s: (npb + b, j))),
                input_output_aliases={6: 0},
                compiler_params=_cp(2),
                name="attn_d_sample",
            )(sink, qd, o1b, o4b, ck, cv, od)
            lhs_a, lhs_b = oc, od

        wr = jnp.concatenate([w_group_router[l], w_expert_router[l]], axis=1)
        n_r = wr.shape[1]
        wr = jnp.pad(wr, ((0, 0), (0, LANES - n_r)))
        wr_hi = wr.astype(BF16)
        wr_cat = jnp.concatenate([wr_hi, (wr - wr_hi.astype(F32)).astype(BF16)], axis=1)
        rb = jnp.pad(jnp.concatenate([b_group_router[l], b_expert_router[l]]), (0, LANES - n_r))[None]
        n_out = d // TN_OUT
        ka = lhs_a.shape[1]
        x, h2, logits = pl.pallas_call(
            functools.partial(_out_kernel, n_out),
            out_shape=(jax.ShapeDtypeStruct((t, d), F32), jax.ShapeDtypeStruct((t, d), BF16),
                       jax.ShapeDtypeStruct((t, LANES), F32)),
            grid=(t // TM_OUT, n_out),
            in_specs=[pl.BlockSpec((TM_OUT, ka), lambda m, n: (m, 0)),
                      pl.BlockSpec((TM_OUT, lhs_b.shape[1]), lambda m, n: (m, 0)),
                      pl.BlockSpec((None, w_out.shape[1], TN_OUT), lambda m, n: (i, 0, n)),
                      pl.BlockSpec((TM_OUT, TN_OUT), lambda m, n: (m, n)),
                      mod_spec(2, TM_OUT, TN_OUT, 1),
                      full((1, d)), mod_spec(3, TM_OUT), mod_spec(4, TM_OUT),
                      full((d, 2 * LANES)), full((1, LANES))],
            out_specs=(pl.BlockSpec((TM_OUT, TN_OUT), lambda m, n: (m, n)),
                       pl.BlockSpec((TM_OUT, d), lambda m, n: (m, 0)),
                       pl.BlockSpec((TM_OUT, LANES), lambda m, n: (m, 0))),
            scratch_shapes=[pltpu.VMEM((n_out, TM_OUT, TN_OUT), F32)],
            compiler_params=_cp(2),
            name="out_proj",
        )(lhs_a, lhs_b, w_out, x, mt, norm2_g[l:l + 1], mt, mt, wr_cat, rb)

        g_logits = logits[:, :N_GROUPS]
        e_logits = logits[:, N_GROUPS:N_GROUPS + N_EXPERTS].reshape(t, N_GROUPS, EXPERTS_PER_GROUP)
        g_prob = jax.nn.softmax(g_logits, axis=-1)
        g_idx = jnp.argmax(g_logits, axis=-1)
        e_in = jnp.take_along_axis(e_logits, g_idx[:, None, None], axis=1)[:, 0]
        top_p, top_i = lax.top_k(jax.nn.softmax(e_in, axis=-1), TOP_K)
        gate = (jnp.take_along_axis(g_prob, g_idx[:, None], axis=1) * top_p
                / jnp.sum(top_p, axis=-1, keepdims=True))
        eid = (g_idx[:, None] * EXPERTS_PER_GROUP + top_i).reshape(-1).astype(jnp.int32)
        n_pairs = t * TOP_K
        onehot = (eid[:, None] == jnp.arange(N_EXPERTS, dtype=jnp.int32)[None, :]).astype(jnp.int32)
        rank = jnp.take_along_axis(jnp.cumsum(onehot, axis=0) - onehot, eid[:, None], axis=1)[:, 0]
        counts = jnp.sum(onehot, axis=0)
        tb, gsz = EXPERT_TILING[l % len(EXPERT_TILING)]
        nblk_e = (counts + tb - 1) // tb
        blk_end = jnp.cumsum(nblk_e)
        blk_start = blk_end - nblk_e
        dest = (blk_start[eid] * tb + rank).astype(jnp.int32)
        n_blocks = (n_pairs + N_EXPERTS * (tb - 1) + tb - 1) // tb
        n_slots = n_blocks * tb
        slot_token = jnp.zeros((n_slots,), jnp.int32).at[dest].set(
            jnp.arange(n_pairs, dtype=jnp.int32) // TOP_K)
        blk_start = blk_start.astype(jnp.int32)
        nblk_e = nblk_e.astype(jnp.int32)
        xs = h2[slot_token]

        n_half = 2
        fc = d_expert // n_half
        any_spec = pl.BlockSpec(memory_space=pl.ANY)
        gt = gsz * tb
        act = pl.pallas_call(
            functools.partial(_up_kernel, tb, gsz),
            out_shape=jax.ShapeDtypeStruct((n_half, n_slots, fc), BF16),
            grid_spec=pltpu.PrefetchScalarGridSpec(
                num_scalar_prefetch=2, grid=(n_half, N_EXPERTS),
                in_specs=[any_spec,
                          pl.BlockSpec((None, None, d, fc), lambda cc, e, b0, nb: (l, e, 0, cc)),
                          pl.BlockSpec((None, None, d, fc), lambda cc, e, b0, nb: (l, e, 0, cc))],
                out_specs=any_spec,
                scratch_shapes=[pltpu.VMEM((d, fc), BF16), pltpu.VMEM((d, fc), BF16),
                                pltpu.VMEM((2, gt, d), BF16), pltpu.VMEM((2, gt, fc), BF16),
                                pltpu.SemaphoreType.DMA((2,)), pltpu.SemaphoreType.DMA((2,))]),
            compiler_params=_cp(2),
            name="expert_up",
        )(blk_start, nblk_e, xs, w_expert_gate, w_expert_up)
        dn = d // 2
        yb = pl.pallas_call(
            functools.partial(_down_kernel, tb, gsz),
            out_shape=jax.ShapeDtypeStruct((n_slots, d), F32),
            grid_spec=pltpu.PrefetchScalarGridSpec(
                num_scalar_prefetch=2, grid=(N_EXPERTS, d // dn),
                in_specs=[any_spec,
                          pl.BlockSpec((None, None, d_expert, dn),
                                       lambda e, hh, b0, nb: (l, e, 0, hh))],
                out_specs=any_spec,
                scratch_shapes=[pltpu.VMEM((d_expert, dn), BF16),
                                pltpu.VMEM((2, gt, d_expert), BF16), pltpu.VMEM((2, gt, dn), F32),
                                pltpu.SemaphoreType.DMA((2, n_half)), pltpu.SemaphoreType.DMA((2,))]),
            compiler_params=_cp(2),
            name="expert_down",
        )(blk_start, nblk_e, act, w_expert_down)

        n_cmb = t // TM_CMB
        dest2 = dest.reshape(n_cmb, TM_CMB, TOP_K)
        tbl = jnp.concatenate([dest2[:, :, k] for k in range(TOP_K)], axis=1)[:, None, :]
        gw = jnp.pad(gate.astype(F32), ((0, 0), (0, LANES - TOP_K)))
        with_next = l + 1 < depth
        npb_c = tp // TM_CMB
        row_blk = pl.BlockSpec((TM_CMB, d), lambda m: (m, 0))
        smem_blk = lambda fn: pl.BlockSpec((None, 1, TOP_K * TM_CMB), fn, memory_space=pltpu.SMEM)
        in_specs = [smem_blk(lambda m: (m, 0, 0)),
                    smem_blk(lambda m: (jnp.minimum(m + 1, n_cmb - 1), 0, 0)),
                    row_blk, any_spec,
                    pl.BlockSpec((TM_CMB, LANES), lambda m: (m, 0)), mod_spec(5, TM_CMB)]
        args = [tbl, tbl, x, yb, gw, mt]
        if with_next:
            in_specs += [full((1, d)), mod_spec(0, TM_CMB), mod_spec(1, TM_CMB)]
            args += [norm1_g[l + 1:l + 2], modtab[l + 1], modtab[l + 1]]
            out_shape = (jax.ShapeDtypeStruct((t, d), F32), jax.ShapeDtypeStruct((t, d), BF16))
            out_specs = (row_blk, row_blk)
        else:
            out_shape = (jax.ShapeDtypeStruct((tp, d), F32), jax.ShapeDtypeStruct((ts, d), F32))
            out_specs = (pl.BlockSpec((TM_CMB, d), lambda m: (jnp.minimum(m, npb_c - 1), 0)),
                         pl.BlockSpec((TM_CMB, d), lambda m: (jnp.maximum(m - npb_c, 0), 0)))
        res = pl.pallas_call(
            functools.partial(_combine_kernel, with_next, npb_c),
            out_shape=out_shape, grid=(n_cmb,), in_specs=in_specs, out_specs=out_specs,
            scratch_shapes=[pltpu.VMEM((2, TOP_K * TM_CMB, d), F32), pltpu.SemaphoreType.DMA((2,))],
            compiler_params=_cp(1),
            name="moe_combine",
        )(*args)
        if with_next:
            x, h = res

    y_prompt, y_sample = res
    return (y_prompt.reshape(batch, seq, d), y_sample.reshape(dec_batch, dec_seq, d),
            jnp.stack(a_k, axis=1), jnp.stack(a_v, axis=1), jnp.stack(c_kv, axis=1),
            jnp.stack(c_pe, axis=1), jnp.stack(d_k, axis=1), jnp.stack(d_v, axis=1))
```

```python
import functools

import jax
import jax.numpy as jnp
import numpy as np
from jax import lax
from jax.experimental import pallas as pl
from jax.experimental.pallas import tpu as pltpu

F32 = jnp.float32
BF16 = jnp.bfloat16

HEAD = 128
LANES = 128
EPS = 1e-6
ROPE_THETA = 10000.0
GRID_W = 64
NEG_INF = -1e30
N_MOD = 6
TOP_K = 2
N_GROUPS = 4
EXPERTS_PER_GROUP = 4
N_EXPERTS = N_GROUPS * EXPERTS_PER_GROUP
WINDOW = 128
VMEM_LIMIT = 56 * 1024 * 1024

TM = 1024
TM_OUT = 512
TN_OUT = 512
TM_CMB = 256
MLA_HEADS_PER_STEP = ((4, 2), (4, 4))
EXPERT_TILING = ((128, 4),)
TN_EVEN = 768
TN_ODD = 1024
MOD_TN = 1024
MOD_ROWS = 16


def _cp(n_axes, vmem=VMEM_LIMIT):
    return pltpu.CompilerParams(dimension_semantics=("arbitrary",) * n_axes,
                                vmem_limit_bytes=vmem)


def _silu(x):
    return x * (1.0 / (1.0 + jnp.exp(-x)))


def _rms_scale(sumsq, n):
    return lax.rsqrt(sumsq * (1.0 / n) + EPS)


def _rope(y, cos, sin, q):
    lane = lax.broadcasted_iota(jnp.int32, y.shape, 1)
    first = jnp.bitwise_and(lane, 2 * q - 1) < q
    sw = jnp.where(first, pltpu.roll(y, LANES - q, 1), pltpu.roll(y, q, 1))
    return y * cos + sw * sin


def _rope_tables(n_tok, rot_dim):
    rows = n_tok // GRID_W
    row_id = jnp.repeat(jnp.arange(rows, dtype=F32), GRID_W)
    col_id = jnp.tile(jnp.arange(GRID_W, dtype=F32), rows)
    axis_dim = rot_dim // 2
    inv_freq = ROPE_THETA ** (-jnp.arange(0, axis_dim, 2, dtype=F32) / axis_dim)
    ang_r = row_id[:, None] * inv_freq[None, :]
    ang_c = col_id[:, None] * inv_freq[None, :]
    cr, sr, cc, sc = jnp.cos(ang_r), jnp.sin(ang_r), jnp.cos(ang_c), jnp.sin(ang_c)
    cos = jnp.concatenate([cr, cr, cc, cc], axis=-1)
    sin = jnp.concatenate([-sr, sr, -sc, sc], axis=-1)
    pad = LANES - rot_dim
    if pad:
        cos = jnp.pad(cos, ((0, 0), (0, pad)))
        sin = jnp.pad(sin, ((0, 0), (0, pad)))
    return cos, sin


def _mod_kernel(c_ref, w_ref, b_ref, o_ref):
    s = _silu(c_ref[...]).astype(BF16)
    o_ref[...] = jnp.dot(s, w_ref[...].astype(BF16), preferred_element_type=F32) + b_ref[...]


def _modulation(cvec, w_mod, b_mod):
    depth, d, n = w_mod.shape
    return pl.pallas_call(
        _mod_kernel,
        out_shape=jax.ShapeDtypeStruct((depth, MOD_ROWS, n), F32),
        grid=(depth, n // MOD_TN),
        in_specs=[pl.BlockSpec((MOD_ROWS, d), lambda l, j: (0, 0)),
                  pl.BlockSpec((None, d, MOD_TN), lambda l, j: (l, 0, j)),
                  pl.BlockSpec((None, 1, MOD_TN), lambda l, j: (l, 0, j))],
        out_specs=pl.BlockSpec((None, MOD_ROWS, MOD_TN), lambda l, j: (l, 0, j)),
        compiler_params=_cp(2),
        name="modulation",
    )(cvec, w_mod, b_mod.reshape(depth, 1, n))


def _adaln(x, g, shift, scale):
    ss = jnp.sum(x * x, axis=-1, keepdims=True)
    y = (x * _rms_scale(ss, x.shape[-1])) * g
    return y * (1.0 + scale) + shift


def _pre_kernel(npb, xp_ref, xs_ref, g_ref, sh_ref, sc_ref, x_ref, h_ref):
    def emit(src_ref):
        x = src_ref[...]
        x_ref[...] = x
        h_ref[...] = _adaln(x, g_ref[...], sh_ref[0], sc_ref[0]).astype(BF16)

    @pl.when(pl.program_id(0) < npb)
    def _():
        emit(xp_ref)

    @pl.when(pl.program_id(0) >= npb)
    def _():
        emit(xs_ref)


def _conv_gate(u, gb, gc, wc_ref, seq):
    z = gc * u
    rows = z.shape[0]
    pos = jnp.bitwise_and(lax.broadcasted_iota(jnp.int32, z.shape, 0), seq - 1)
    zp = jnp.where(pos == 0, 0.0, pltpu.roll(z, 1, 0))
    zn = jnp.where(pos == seq - 1, 0.0, pltpu.roll(z, rows - 1, 0))
    w = wc_ref[...]
    y = zp * w[0:1, :] + z * w[1:2, :] + zn * w[2:3, :]
    return gb * y


def _in_even_kernel(npb, seq_p, seq_s, n_qkv, n_normed, n_kvf0,
                    h_ref, wa_ref, wb_ref, wc3_ref, gcol_ref, cos_ref, sin_ref, wc_ref,
                    qkv_ref, kvf_ref, b_ref):
    m = pl.program_id(0)
    n = pl.program_id(1)
    h = h_ref[...]
    accs = [jnp.dot(h, w[...], preferred_element_type=F32) for w in (wa_ref, wb_ref, wc3_ref)]
    piece = accs[0].shape[1]
    per = piece // HEAD
    heads = 3 * per

    def qkv_epilogue(sample):
        ys = []
        for j in range(heads):
            sl = slice(j * HEAD, (j + 1) * HEAD)
            a = accs[j // per][:, (j % per) * HEAD:(j % per + 1) * HEAD]
            normed = n * heads + j < n_normed
            ss = jnp.sum(a * a, axis=-1, keepdims=True)
            y = (a * _rms_scale(ss, HEAD)) * gcol_ref[:, sl]
            if sample:
                y = _rope(y, cos_ref[...], sin_ref[...], HEAD // 4)
            y = jnp.where(normed, y, a)
            qkv_ref[:, sl] = y.astype(BF16)
            ys.append(y)

        @pl.when(n >= n_kvf0)
        def _():
            for j in range(heads):
                kvf_ref[:, j * HEAD:(j + 1) * HEAD] = ys[j]

    @pl.when(jnp.logical_and(n < n_qkv, m < npb))
    def _():
        qkv_epilogue(False)

    @pl.when(jnp.logical_and(n < n_qkv, m >= npb))
    def _():
        qkv_epilogue(True)

    @pl.when(jnp.logical_and(n >= n_qkv, m < npb))
    def _():
        b_ref[...] = _conv_gate(accs[0], accs[1], accs[2], wc_ref, seq_p).astype(BF16)

    @pl.when(jnp.logical_and(n >= n_qkv, m >= npb))
    def _():
        b_ref[...] = _conv_gate(accs[0], accs[1], accs[2], wc_ref, seq_s).astype(BF16)


def _in_odd_kernel(npb, kv_lora,
                   h_ref, w_ref, gcol_ref, cos_ref, sin_ref,
                   cqn_ref, o1b_ref, o1f_ref, qd_ref, o4b_ref, o4f_ref):
    m = pl.program_id(0)
    n = pl.program_id(1)
    acc = jnp.dot(h_ref[...], w_ref[...], preferred_element_type=F32)
    width = acc.shape[1]

    def head_norm(a, sl, sample):
        ss = jnp.sum(a * a, axis=-1, keepdims=True)
        y = (a * _rms_scale(ss, HEAD)) * gcol_ref[:, sl]
        if sample:
            y = _rope(y, cos_ref[...], sin_ref[...], HEAD // 4)
        return y

    @pl.when(n == 0)
    def _():
        ss = jnp.sum(acc * acc, axis=-1, keepdims=True)
        cqn_ref[...] = ((acc * _rms_scale(ss, width)) * gcol_ref[...]).astype(BF16)

    def block1(sample):
        a = acc[:, :kv_lora]
        ss = jnp.sum(a * a, axis=-1, keepdims=True)
        y = (a * _rms_scale(ss, kv_lora)) * gcol_ref[:, :kv_lora]
        o1f_ref[:, :kv_lora] = y
        o1b_ref[:, :kv_lora] = y.astype(BF16)
        for j in range(kv_lora // HEAD, width // HEAD):
            sl = slice(j * HEAD, (j + 1) * HEAD)
            y = head_norm(acc[:, sl], sl, sample)
            o1f_ref[:, sl] = y
            o1b_ref[:, sl] = y.astype(BF16)

    def blockq(sample):
        for j in range(width // HEAD):
            sl = slice(j * HEAD, (j + 1) * HEAD)
            qd_ref[:, sl] = head_norm(acc[:, sl], sl, sample).astype(BF16)

    @pl.when(jnp.logical_and(n == 1, m < npb))
    def _():
        block1(False)

    @pl.when(jnp.logical_and(n == 1, m >= npb))
    def _():
        block1(True)

    is_q = jnp.logical_or(n == 2, n == 3)

    @pl.when(jnp.logical_and(is_q, m < npb))
    def _():
        blockq(False)

    @pl.when(jnp.logical_and(is_q, m >= npb))
    def _():
        blockq(True)

    @pl.when(n == 4)
    def _():
        o4f_ref[...] = acc
        o4b_ref[...] = acc.astype(BF16)


def _qb_kernel(npb, n_heads, qk_head,
               x_ref, w_ref, g_ref, cos_ref, sin_ref, q_ref):
    m = pl.program_id(0)
    acc = jnp.dot(x_ref[...], w_ref[...], preferred_element_type=F32)

    def body(sample):
        for h in range(n_heads):
            s0 = slice(2 * h * HEAD, (2 * h + 1) * HEAD)
            s1 = slice((2 * h + 1) * HEAD, (2 * h + 2) * HEAD)
            nope, pe = acc[:, s0], acc[:, s1]
            ss = jnp.sum(nope * nope + pe * pe, axis=-1, keepdims=True)
            r = _rms_scale(ss, qk_head)
            q_ref[:, s0] = ((nope * r) * g_ref[:, s0]).astype(BF16)
            y = (pe * r) * g_ref[:, s1]
            if sample:
                y = _rope(y, cos_ref[...], sin_ref[...], HEAD // 8)
            q_ref[:, s1] = y.astype(BF16)

    @pl.when(m < npb)
    def _():
        body(False)

    @pl.when(m >= npb)
    def _():
        body(True)


def _kvb_kernel(lo_s, hi_s, n_heads, qk_head,
                x_ref, pe_ref, w_ref, g_ref, cos_ref, sin_ref, k_ref, v_ref):
    m = pl.program_id(0)
    acc = jnp.dot(x_ref[...], w_ref[...], preferred_element_type=F32)
    kpe = pe_ref[...]
    ss_pe = jnp.sum(kpe * kpe, axis=-1, keepdims=True)
    nk = n_heads * HEAD

    def body(sample):
        pe_g = kpe * g_ref[:, HEAD:2 * HEAD]
        if sample:
            pe_g = _rope(pe_g, cos_ref[...], sin_ref[...], HEAD // 8)
        for h in range(n_heads):
            kn = acc[:, h * HEAD:(h + 1) * HEAD]
            ss = jnp.sum(kn * kn, axis=-1, keepdims=True) + ss_pe
            r = _rms_scale(ss, qk_head)
            k_ref[:, 2 * h * HEAD:(2 * h + 1) * HEAD] = ((kn * r) * g_ref[:, :HEAD]).astype(BF16)
            k_ref[:, (2 * h + 1) * HEAD:(2 * h + 2) * HEAD] = (pe_g * r).astype(BF16)
            v_ref[:, h * HEAD:(h + 1) * HEAD] = acc[:, nk + h * HEAD:nk + (h + 1) * HEAD].astype(BF16)

    is_s = jnp.logical_and(m >= lo_s, m < hi_s)

    @pl.when(jnp.logical_not(is_s))
    def _():
        body(False)

    @pl.when(is_s)
    def _():
        body(True)


def _softmax_pv(s, v, sink=None):
    mx = jnp.max(s, axis=-1, keepdims=True)
    if sink is not None:
        mx = jnp.maximum(mx, sink)
    p = jnp.exp(s - mx)
    l = jnp.sum(p, axis=-1, keepdims=True)
    if sink is not None:
        l = l + jnp.exp(sink - mx)
    p = p * (1.0 / l)
    return jnp.dot(p.astype(BF16), v, preferred_element_type=F32)


def _qk(q, k):
    return lax.dot_general(q, k, (((1,), (1,)), ((), ())), preferred_element_type=F32)


def _attn_prompt_kernel(n_seq, seq, n_kv, groups, dk, scale, use_sink, *refs):
    if use_sink:
        sink_ref, q_ref, k_ref, v_ref, o_ref = refs
    else:
        q_ref, k_ref, v_ref, o_ref = refs
    j = pl.program_id(1)
    dv = v_ref.shape[1] // n_kv
    for b in range(n_seq):
        rows = slice(b * seq, (b + 1) * seq)
        for c in range(n_kv):
            k = k_ref[rows, c * dk:(c + 1) * dk]
            v = v_ref[rows, c * dv:(c + 1) * dv]
            for g in range(groups):
                hq = c * groups + g
                q = q_ref[rows, hq * dk:(hq + 1) * dk]
                s = _qk(q, k) * scale
                sink = sink_ref[(j * n_kv + c) * groups + g] if use_sink else None
                o_ref[rows, hq * dv:(hq + 1) * dv] = _softmax_pv(s, v, sink).astype(o_ref.dtype)


def _attn_sample_kernel(seq, qblk, n_kv, groups, dk, scale, *refs):
    q_ref, k_ref, v_ref, ck_ref, cv_ref, a_ref, o_ref = refs
    del a_ref
    dv = v_ref.shape[1] // n_kv
    ks = [jnp.concatenate([k_ref[:, c * dk:(c + 1) * dk],
                           ck_ref[:, c * dk:(c + 1) * dk].astype(BF16)], axis=0) for c in range(n_kv)]
    vs = [jnp.concatenate([v_ref[:, c * dv:(c + 1) * dv],
                           cv_ref[:, c * dv:(c + 1) * dv].astype(BF16)], axis=0) for c in range(n_kv)]

    def step(i, carry):
        r0 = pl.multiple_of(i * qblk, qblk)
        for c in range(n_kv):
            for g in range(groups):
                hq = c * groups + g
                q = q_ref[pl.ds(r0, qblk), hq * dk:(hq + 1) * dk]
                s = _qk(q, ks[c]) * scale
                o_ref[pl.ds(r0, qblk), hq * dv:(hq + 1) * dv] = (
                    _softmax_pv(s, vs[c]).astype(o_ref.dtype))
        return carry

    lax.fori_loop(0, seq // qblk, step, 0)


def _attn_window_kernel(seq, groups, scale, sink_ref, q_ref, k_ref, v_ref, ck_ref, cv_ref,
                        a_ref, o_ref):
    del a_ref
    j = pl.program_id(1)
    ck = ck_ref[...].astype(BF16)
    cv = cv_ref[...].astype(BF16)
    w = WINDOW
    rid = lax.shift_right_logical(lax.broadcasted_iota(jnp.int32, (groups * w, 1), 0),
                                  int(np.log2(w)))
    sink = jnp.zeros((groups * w, 1), F32)
    for g in range(groups):
        sink = jnp.where(rid == g, sink_ref[j * groups + g], sink)
    for qb in range(seq // w):
        lo = max(0, (qb - 1) * w)
        hi = min(seq, (qb + 2) * w)
        rows = slice(qb * w, (qb + 1) * w)
        q = jnp.concatenate([q_ref[rows, g * HEAD:(g + 1) * HEAD] for g in range(groups)], axis=0)
        kl = k_ref[lo:hi, :]
        vl = v_ref[lo:hi, :]
        s_loc = _qk(q, kl) * scale
        qpos = qb * w + jnp.bitwise_and(lax.broadcasted_iota(jnp.int32, s_loc.shape, 0), w - 1)
        kpos = lo + lax.broadcasted_iota(jnp.int32, s_loc.shape, 1)
        s_loc = jnp.where(jnp.abs(qpos - kpos) <= w, s_loc, NEG_INF)
        s_ctx = _qk(q, ck) * scale
        mx = jnp.maximum(jnp.maximum(jnp.max(s_loc, axis=-1, keepdims=True),
                                     jnp.max(s_ctx, axis=-1, keepdims=True)), sink)
        p_loc = jnp.exp(s_loc - mx)
        p_ctx = jnp.exp(s_ctx - mx)
        l = (jnp.sum(p_loc, axis=-1, keepdims=True) + jnp.sum(p_ctx, axis=-1, keepdims=True)
             + jnp.exp(sink - mx))
        inv = 1.0 / l
        o = (jnp.dot((p_loc * inv).astype(BF16), vl, preferred_element_type=F32)
             + jnp.dot((p_ctx * inv).astype(BF16), cv, preferred_element_type=F32))
        for g in range(groups):
            o_ref[rows, g * HEAD:(g + 1) * HEAD] = o[g * w:(g + 1) * w, :].astype(o_ref.dtype)


def _out_kernel(n_chunks, a_ref, b_ref, w_ref, x_ref, gate_ref, g2_ref, sh_ref, sc_ref,
                wr_ref, rb_ref, xo_ref, h2_ref, lg_ref, xs_ref):
    n = pl.program_id(1)
    ka = a_ref.shape[1]
    acc = (jnp.dot(a_ref[...], w_ref[:ka, :], preferred_element_type=F32)
           + jnp.dot(b_ref[...], w_ref[ka:, :], preferred_element_type=F32))
    xn = x_ref[...] + gate_ref[0] * acc
    xo_ref[...] = xn
    xs_ref[n] = xn

    @pl.when(n == n_chunks - 1)
    def _():
        tn = xn.shape[1]
        ss = jnp.zeros((xn.shape[0], 1), F32)
        for c in range(n_chunks):
            xc = xs_ref[c]
            ss = ss + jnp.sum(xc * xc, axis=-1, keepdims=True)
        r = _rms_scale(ss, n_chunks * tn)
        lg = jnp.zeros(lg_ref.shape, F32) + rb_ref[...]
        for c in range(n_chunks):
            sl = slice(c * tn, (c + 1) * tn)
            y = (xs_ref[c] * r) * g2_ref[:, sl]
            h = y * (1.0 + sc_ref[0][:, sl]) + sh_ref[0][:, sl]
            hi = h.astype(BF16)
            lo = (h - hi.astype(F32)).astype(BF16)
            h2_ref[:, sl] = hi
            both = jnp.dot(hi, wr_ref[sl, :], preferred_element_type=F32)
            lg = lg + (both[:, :LANES] + both[:, LANES:]
                       + jnp.dot(lo, wr_ref[sl, :LANES], preferred_element_type=F32))
        lg_ref[...] = lg


def _route_kernel(lg_ref, gate_ref, idx_ref, cnt_ref, carry_ref):
    @pl.when(pl.program_id(0) == 0)
    def _():
        carry_ref[...] = jnp.zeros(carry_ref.shape, F32)

    lg = lg_ref[...]
    tm = lg.shape[0]
    lane = lax.broadcasted_iota(jnp.int32, lg.shape, 1)

    def first_lane(mask):
        return jnp.min(jnp.where(mask, lane, LANES), axis=-1, keepdims=True)

    gl = jnp.where(lane < N_GROUPS, lg, NEG_INF)
    gmax = jnp.max(gl, axis=-1, keepdims=True)
    g_idx = first_lane(gl == gmax)
    g_sel = 1.0 / jnp.sum(jnp.exp(gl - gmax), axis=-1, keepdims=True)

    lo = N_GROUPS + g_idx * EXPERTS_PER_GROUP
    in_group = jnp.logical_and(lane >= lo, lane < lo + EXPERTS_PER_GROUP)
    el = jnp.where(in_group, lg, NEG_INF)
    pe = jnp.exp(el - jnp.max(el, axis=-1, keepdims=True))
    probs = pe / jnp.sum(pe, axis=-1, keepdims=True)
    probs = jnp.where(in_group, probs, -1.0)
    p1 = jnp.max(probs, axis=-1, keepdims=True)
    i1 = first_lane(probs == p1)
    probs2 = jnp.where(lane == i1, -1.0, probs)
    p2 = jnp.max(probs2, axis=-1, keepdims=True)
    i2 = first_lane(probs2 == p2)
    denom = p1 + p2
    gate_ref[...] = jnp.where(lane == 0, g_sel * p1 / denom,
                              jnp.where(lane == 1, g_sel * p2 / denom, 0.0))

    e0 = i1 - N_GROUPS
    e1 = i2 - N_GROUPS
    member = jnp.logical_or(lane == e0, lane == e1).astype(F32)
    r = lax.broadcasted_iota(jnp.int32, (tm, tm), 0)
    c = lax.broadcasted_iota(jnp.int32, (tm, tm), 1)
    earlier = (r > c).astype(BF16)
    before = jnp.dot(earlier, member.astype(BF16), preferred_element_type=F32) + carry_ref[0:1, :]
    r0 = jnp.sum(jnp.where(lane == e0, before, 0.0), axis=-1, keepdims=True).astype(jnp.int32)
    r1 = jnp.sum(jnp.where(lane == e1, before, 0.0), axis=-1, keepdims=True).astype(jnp.int32)
    idx_ref[...] = jnp.where(lane == 0, e0, jnp.where(lane == 1, e1,
                             jnp.where(lane == 2, r0, jnp.where(lane == 3, r1, 0))))
    total = carry_ref[...] + jnp.sum(member, axis=0, keepdims=True)
    carry_ref[...] = total
    cnt_ref[...] = total


def _group_pipeline(nblk, gsz, in_copies, out_copy, compute):
    shift = int(np.log2(gsz))
    ngrp = lax.shift_right_logical(nblk + (gsz - 1), shift)

    def count(g):
        return jnp.minimum(gsz, nblk - g * gsz)

    def each_in(g, slot, fn):
        c = count(g)
        for k in range(gsz):
            @pl.when(k < c)
            def _():
                for cp in in_copies(g * gsz + k, slot, k):
                    fn(cp)

    def by_count(c, fn):
        for s in range(1, gsz + 1):
            @pl.when(c == s)
            def _():
                fn(s)

    @pl.when(nblk > 0)
    def _():
        each_in(0, 0, lambda cp: cp.start())

        def body(g, carry):
            slot = jnp.bitwise_and(g, 1)

            @pl.when(g + 1 < ngrp)
            def _():
                each_in(g + 1, 1 - slot, lambda cp: cp.start())

            each_in(g, slot, lambda cp: cp.wait())

            @pl.when(g >= 2)
            def _():
                out_copy(g - 2, slot, gsz).wait()

            def run(s):
                compute(slot, s)
                out_copy(g, slot, s).start()

            by_count(count(g), run)
            return carry

        lax.fori_loop(0, ngrp, body, 0)

        @pl.when(ngrp >= 2)
        def _():
            out_copy(ngrp - 2, jnp.bitwise_and(ngrp, 1), gsz).wait()

        last = ngrp - 1
        by_count(count(last), lambda s: out_copy(last, jnp.bitwise_and(last, 1), s).wait())


def _up_kernel(tb, gsz, b0_ref, nb_ref, x_hbm, wg_ref, wu_ref, a_hbm,
               wg_s, wu_s, xbuf, obuf, sem_in, sem_out):
    cc = pl.program_id(0)
    e = pl.program_id(1)
    first = b0_ref[e]

    def rows(blk, n):
        return pl.ds(pl.multiple_of((first + blk) * tb, tb), n * tb)

    def in_copies(blk, slot, k):
        return [pltpu.make_async_copy(x_hbm.at[rows(blk, 1), :],
                                      xbuf.at[slot, k * tb:(k + 1) * tb], sem_in.at[slot])]

    def out_copy(g, slot, c):
        return pltpu.make_async_copy(obuf.at[slot, :c * tb], a_hbm.at[cc, rows(g * gsz, c), :],
                                     sem_out.at[slot])

    def compute(slot, c):
        x = xbuf[slot, :c * tb]
        g = jnp.dot(x, wg_s[...], preferred_element_type=F32)
        u = jnp.dot(x, wu_s[...], preferred_element_type=F32)
        obuf[slot, :c * tb] = (_silu(g) * u).astype(BF16)

    wg_s[...] = wg_ref[...].astype(BF16)
    wu_s[...] = wu_ref[...].astype(BF16)
    _group_pipeline(nb_ref[e], gsz, in_copies, out_copy, compute)


def _down_kernel(tb, gsz, b0_ref, nb_ref, a_hbm, wd_ref, y_hbm, wd_s, abuf, ybuf, sem_in, sem_out):
    e = pl.program_id(0)
    half = pl.program_id(1)
    first = b0_ref[e]
    n_half, _, fc = a_hbm.shape
    dn = wd_s.shape[1]

    def rows(blk, n):
        return pl.ds(pl.multiple_of((first + blk) * tb, tb), n * tb)

    def in_copies(blk, slot, k):
        return [pltpu.make_async_copy(a_hbm.at[j, rows(blk, 1), :],
                                      abuf.at[slot, k * tb:(k + 1) * tb, j * fc:(j + 1) * fc],
                                      sem_in.at[slot, j])
                for j in range(n_half)]

    def out_copy(hh, g, slot, c):
        return pltpu.make_async_copy(ybuf.at[slot, :c * tb],
                                     y_hbm.at[rows(g * gsz, c), hh * dn:(hh + 1) * dn],
                                     sem_out.at[slot])

    def compute(slot, c):
        ybuf[slot, :c * tb] = jnp.dot(abuf[slot, :c * tb], wd_s[...], preferred_element_type=F32)

    wd_s[...] = wd_ref[...].astype(BF16)
    for hh in range(y_hbm.shape[1] // dn):
        @pl.when(half == hh)
        def _():
            _group_pipeline(nb_ref[e], gsz, in_copies, functools.partial(out_copy, hh), compute)


def _combine_kernel(with_next, npb, tbl_ref, tbl_next_ref, x_ref, y_hbm, gw_ref, gate_ref, *refs):
    if with_next:
        g_ref, sh_ref, sc_ref, xo_ref, h_ref, ybuf, sem = refs
    else:
        xp_ref, xs_ref, ybuf, sem = refs
    m = pl.program_id(0)
    n_steps = pl.num_programs(0)
    n_rows = ybuf.shape[1]
    tm = n_rows // TOP_K
    slot = jnp.bitwise_and(m, 1)

    def row_copy(tbl, r, s):
        return pltpu.make_async_copy(y_hbm.at[pl.ds(tbl[0, r], 1), :], ybuf.at[s, pl.ds(r, 1), :],
                                     sem.at[s])

    def start_all(tbl, s):
        def body(r, carry):
            row_copy(tbl, r, s).start()
            return carry
        lax.fori_loop(0, n_rows, body, 0, unroll=8)

    def wait_all(tbl, s):
        def body(r, carry):
            row_copy(tbl, r, s).wait()
            return carry
        lax.fori_loop(0, n_rows, body, 0, unroll=8)

    @pl.when(m == 0)
    def _():
        start_all(tbl_ref, 0)

    @pl.when(m + 1 < n_steps)
    def _():
        start_all(tbl_next_ref, 1 - slot)

    wait_all(tbl_ref, slot)
    gw = gw_ref[...]
    yb = ybuf.at[slot]
    moe = gw[:, 0:1] * yb[:tm, :] + gw[:, 1:2] * yb[tm:, :]
    xn = x_ref[...] + gate_ref[0] * moe
    if with_next:
        xo_ref[...] = xn
        h_ref[...] = _adaln(xn, g_ref[...], sh_ref[0], sc_ref[0]).astype(BF16)
    else:
        @pl.when(m < npb)
        def _():
            xp_ref[...] = xn

        @pl.when(m >= npb)
        def _():
            xs_ref[...] = xn


def kernel(x_prompt, x_sample, cache_a_k, cache_a_v, cache_c_kv, cache_c_pe, cache_d_k, cache_d_v,
           c, c_ctx, w_mod, b_mod, norm1_g, norm2_g, w_in_even, w_out_even, a_q_norm, a_k_norm,
           b_conv, w_in_odd, w_out_odd, c_q_a_norm, c_kv_a_norm, c_w_q_b, c_w_kv_b, c_q_norm,
           c_k_norm, d_q_norm, d_k_norm, d_sink, w_group_router, b_group_router, w_expert_router,
           b_expert_router, w_expert_gate, w_expert_up, w_expert_down):
    batch, seq, d = x_prompt.shape
    dec_batch, dec_seq, _ = x_sample.shape
    past = cache_a_k.shape[2]
    depth = w_mod.shape[0]
    kvh = cache_a_k.shape[3]
    b_width = d // 2
    a_width = w_out_even.shape[1] - b_width
    a_heads = a_width // HEAD
    groups = a_heads // kvh
    kv_width = kvh * HEAD
    q_lora = c_q_a_norm.shape[1]
    kv_lora = c_kv_a_norm.shape[1]
    qk_head = c_q_norm.shape[1]
    qk_rope = cache_c_pe.shape[3]
    qk_nope = qk_head - qk_rope
    c_heads = c_w_q_b.shape[2] // qk_head
    v_dim = c_w_kv_b.shape[2] // c_heads - qk_nope
    d_width = d_sink.shape[1] * HEAD
    d_expert = w_expert_gate.shape[3]
    tp = batch * seq
    ts = dec_batch * dec_seq
    t = tp + ts
    assert qk_nope == HEAD and v_dim == HEAD and 2 * qk_rope == HEAD and d_width == a_width
    assert seq == past and dec_seq == TM and tp % TM == 0 and TM % seq == 0

    def row_of(tm):
        npb, bps = tp // tm, dec_seq // tm
        return lambda m: jnp.where(m < npb, 0, 1 + (m - npb) // bps)

    cvec = jnp.zeros((MOD_ROWS, d), F32).at[0].set(c_ctx).at[1:1 + dec_batch].set(c)
    mods = _modulation(cvec, w_mod, b_mod)
    n_rows = 1 + dec_batch
    modtab = (mods[:, :n_rows].reshape(depth, n_rows, N_MOD, d).transpose(0, 2, 1, 3)
              .reshape(depth, N_MOD * n_rows, 1, d))

    def mod_spec(j, tm, cols=None, col_axis=None):
        r = row_of(tm)
        if cols is None:
            return pl.BlockSpec((1, 1, d), lambda *g: (j * n_rows + r(g[0]), 0, 0))
        return pl.BlockSpec((1, 1, cols), lambda *g: (j * n_rows + r(g[0]), 0, g[col_axis]))

    cos_hd, sin_hd = _rope_tables(dec_seq, HEAD)
    cos_pe, sin_pe = _rope_tables(dec_seq, qk_rope)

    npb = tp // TM
    npb_o = tp // TM_CMB

    full = lambda shape: pl.BlockSpec(shape, lambda *g: (0,) * len(shape))

    x, h = pl.pallas_call(
        functools.partial(_pre_kernel, npb_o),
        out_shape=(jax.ShapeDtypeStruct((t, d), F32), jax.ShapeDtypeStruct((t, d), BF16)),
        grid=(t // TM_CMB,),
        in_specs=[pl.BlockSpec((TM_CMB, d), lambda m: (jnp.minimum(m, npb_o - 1), 0)),
                  pl.BlockSpec((TM_CMB, d), lambda m: (jnp.maximum(m - npb_o, 0), 0)),
                  full((1, d)), mod_spec(0, TM_CMB), mod_spec(1, TM_CMB)],
        out_specs=(pl.BlockSpec((TM_CMB, d), lambda m: (m, 0)),
                   pl.BlockSpec((TM_CMB, d), lambda m: (m, 0))),
        compiler_params=_cp(1),
        name="adaln_first",
    )(x_prompt.reshape(tp, d), x_sample.reshape(ts, d), norm1_g[0:1], modtab[0], modtab[0])

    a_k, a_v, c_kv, c_pe, d_k, d_v = [], [], [], [], [], []
    w_in_even16 = w_in_even.astype(BF16)
    w_out16 = (w_out_even.astype(BF16), w_out_odd.astype(BF16))

    for l in range(depth):
        i = l // 2
        mt = modtab[l]
        w_out = w_out16[l % 2]
        if l % 2 == 0:
            w_in = w_in_even16
            qkv_w = a_width + 2 * kv_width
            cch = TN_EVEN // 3
            n_qkv = qkv_w // TN_EVEN
            pb = qkv_w // cch
            nch = b_width // cch
            n_kvf0 = a_width // TN_EVEN
            kvf_off = a_width - n_kvf0 * TN_EVEN
            gcol = jnp.concatenate([jnp.tile(a_q_norm[i], a_heads), jnp.tile(a_k_norm[i], kvh),
                                    jnp.ones((kv_width,), F32)])[None, :]
            n_in = n_qkv + nch

            def piece(k):
                return pl.BlockSpec((None, d, cch), lambda m, n: (
                    i, 0, jnp.where(n < n_qkv, 3 * n + k, pb + k * nch + n - n_qkv)))

            qkv, kvf, bconv = pl.pallas_call(
                functools.partial(_in_even_kernel, npb, seq, dec_seq, n_qkv, a_heads + kvh, n_kvf0),
                out_shape=(jax.ShapeDtypeStruct((t, qkv_w), BF16),
                           jax.ShapeDtypeStruct((t, (n_qkv - n_kvf0) * TN_EVEN), F32),
                           jax.ShapeDtypeStruct((t, b_width), BF16)),
                grid=(t // TM, n_in),
                in_specs=[pl.BlockSpec((TM, d), lambda m, n: (m, 0)),
                          piece(0), piece(1), piece(2),
                          pl.BlockSpec((1, TN_EVEN), lambda m, n: (0, jnp.minimum(n, n_qkv - 1))),
                          full((TM, LANES)), full((TM, LANES)),
                          pl.BlockSpec((3, cch), lambda m, n: (0, jnp.maximum(n - n_qkv, 0)))],
                out_specs=(pl.BlockSpec((TM, TN_EVEN), lambda m, n: (m, jnp.minimum(n, n_qkv - 1))),
                           pl.BlockSpec((TM, TN_EVEN),
                                        lambda m, n: (m, jnp.clip(n - n_kvf0, 0, n_qkv - n_kvf0 - 1))),
                           pl.BlockSpec((TM, cch), lambda m, n: (m, jnp.maximum(n - n_qkv, 0)))),
                compiler_params=_cp(2),
                name="in_proj_even",
            )(h, w_in, w_in, w_in, gcol, cos_hd, sin_hd, b_conv[i])
            a_k.append(kvf[:tp, kvf_off:kvf_off + kv_width].reshape(batch, seq, kvh, HEAD))
            a_v.append(kvf[:tp, kvf_off + kv_width:kvf_off + 2 * kv_width]
                       .reshape(batch, seq, kvh, HEAD))

            gw_ = groups * HEAD
            qb0, kb, vb = 0, a_width // HEAD, (a_width + kv_width) // HEAD
            scale = HEAD ** -0.5
            att = pl.pallas_call(
                functools.partial(_attn_prompt_kernel, TM // seq, seq, 1, groups, HEAD, scale, False),
                out_shape=jax.ShapeDtypeStruct((t, a_width), BF16),
                grid=(npb, kvh),
                in_specs=[pl.BlockSpec((TM, gw_), lambda m, j: (m, qb0 + j)),
                          pl.BlockSpec((TM, HEAD), lambda m, j: (m, kb + j)),
                          pl.BlockSpec((TM, HEAD), lambda m, j: (m, vb + j))],
                out_specs=pl.BlockSpec((TM, gw_), lambda m, j: (m, j)),
                compiler_params=_cp(2),
                name="attn_a_prompt",
            )(qkv, qkv, qkv)
            ck = cache_a_k[:, i].reshape(dec_batch, past, kv_width)
            cv = cache_a_v[:, i].reshape(dec_batch, past, kv_width)
            att = pl.pallas_call(
                functools.partial(_attn_sample_kernel, dec_seq, 256, 1, groups, HEAD, scale),
                out_shape=jax.ShapeDtypeStruct((t, a_width), BF16),
                grid=(dec_batch, kvh),
                in_specs=[pl.BlockSpec((TM, gw_), lambda b, j: (npb + b, qb0 + j)),
                          pl.BlockSpec((TM, HEAD), lambda b, j: (npb + b, kb + j)),
                          pl.BlockSpec((TM, HEAD), lambda b, j: (npb + b, vb + j)),
                          pl.BlockSpec((None, past, HEAD), lambda b, j: (b, 0, j)),
                          pl.BlockSpec((None, past, HEAD), lambda b, j: (b, 0, j)),
                          pl.BlockSpec(memory_space=pl.ANY)],
                out_specs=pl.BlockSpec((TM, gw_), lambda b, j: (npb + b, j)),
                input_output_aliases={5: 0},
                compiler_params=_cp(2),
                name="attn_a_sample",
            )(qkv, qkv, qkv, ck, cv, att)
            lhs_a, lhs_b = att, bconv
        else:
            w = w_in_odd[i]
            o = np.cumsum([0, q_lora, kv_lora, qk_rope, d_width, kv_width, kv_width])
            cq_w, ckv_w, kpe_w, dq_w, dk_w, dv_w = [w[:, o[s]:o[s + 1]] for s in range(6)]
            tail = TN_ODD - kv_width - qk_rope
            w_in = jnp.concatenate([cq_w, ckv_w, dk_w, dq_w, dv_w, kpe_w,
                                    jnp.zeros((d, tail), F32)], axis=1).astype(BF16)
            assert q_lora == TN_ODD and kv_lora + kv_width == TN_ODD and d_width == 2 * TN_ODD
            gcol = jnp.concatenate([c_q_a_norm[i], c_kv_a_norm[i], jnp.tile(d_k_norm[i], kvh),
                                    jnp.tile(d_q_norm[i], d_width // HEAD),
                                    jnp.ones((TN_ODD,), F32)])[None, :]
            n_in = w_in.shape[1] // TN_ODD
            blk = lambda fn: pl.BlockSpec((TM_OUT, TN_ODD), fn)
            spb = dec_seq // TM_OUT
            cqn, o1b, o1f, qd, o4b, o4f = pl.pallas_call(
                functools.partial(_in_odd_kernel, tp // TM_OUT, kv_lora),
                out_shape=(jax.ShapeDtypeStruct((t, TN_ODD), BF16),
                           jax.ShapeDtypeStruct((t, TN_ODD), BF16),
                           jax.ShapeDtypeStruct((t, TN_ODD), F32),
                           jax.ShapeDtypeStruct((t, d_width), BF16),
                           jax.ShapeDtypeStruct((t, TN_ODD), BF16),
                           jax.ShapeDtypeStruct((t, TN_ODD), F32)),
                grid=(t // TM_OUT, n_in),
                in_specs=[pl.BlockSpec((TM_OUT, d), lambda m, n: (m, 0)),
                          pl.BlockSpec((d, TN_ODD), lambda m, n: (0, n)),
                          pl.BlockSpec((1, TN_ODD), lambda m, n: (0, n)),
                          pl.BlockSpec((TM_OUT, LANES), lambda m, n: (m % spb, 0)),
                          pl.BlockSpec((TM_OUT, LANES), lambda m, n: (m % spb, 0))],
                out_specs=(blk(lambda m, n: (m, 0)), blk(lambda m, n: (m, 0)),
                           blk(lambda m, n: (m, 0)),
                           blk(lambda m, n: (m, jnp.clip(n - 2, 0, 1))),
                           blk(lambda m, n: (m, 0)), blk(lambda m, n: (m, 0))),
                compiler_params=_cp(2),
                name="in_proj_odd",
            )(h, w_in, gcol, cos_hd, sin_hd)
            c_kv.append(o1f[:tp, :kv_lora].reshape(batch, seq, kv_lora))
            d_k.append(o1f[:tp, kv_lora:].reshape(batch, seq, kvh, HEAD))
            d_v.append(o4f[:tp, :kv_width].reshape(batch, seq, kvh, HEAD))
            c_pe.append(o4f[:tp, kv_width:kv_width + qk_rope].reshape(batch, seq, qk_rope))

            wq = c_w_q_b[i].reshape(q_lora, c_heads, qk_head)
            wq = jnp.pad(wq, ((0, 0), (0, 0), (0, 2 * HEAD - qk_head)))
            wq = wq.reshape(q_lora, c_heads * 2 * HEAD).astype(BF16)
            gq = jnp.tile(jnp.pad(c_q_norm[i], (0, 2 * HEAD - qk_head)), c_heads)[None, :]
            cw = c_heads * 2 * HEAD
            q_c = pl.pallas_call(
                functools.partial(_qb_kernel, tp // TM_OUT, c_heads, qk_head),
                out_shape=jax.ShapeDtypeStruct((t, cw), BF16),
                grid=(t // TM_OUT,),
                in_specs=[pl.BlockSpec((TM_OUT, q_lora), lambda m: (m, 0)),
                          full((q_lora, cw)), full((1, cw)),
                          pl.BlockSpec((TM_OUT, LANES), lambda m: (m % (dec_seq // TM_OUT), 0)),
                          pl.BlockSpec((TM_OUT, LANES), lambda m: (m % (dec_seq // TM_OUT), 0))],
                out_specs=pl.BlockSpec((TM_OUT, cw), lambda m: (m, 0)),
                compiler_params=_cp(1),
                name="mla_q_up",
            )(cqn, wq, gq, cos_pe, sin_pe)

            wkv = c_w_kv_b[i].reshape(kv_lora, c_heads, qk_nope + v_dim)
            wkv = jnp.concatenate([wkv[:, :, :qk_nope].reshape(kv_lora, c_heads * qk_nope),
                                   wkv[:, :, qk_nope:].reshape(kv_lora, c_heads * v_dim)],
                                  axis=1).astype(BF16)
            n_ctx = dec_batch * past
            ckv_all = jnp.concatenate([o1b[:, :kv_lora],
                                       cache_c_kv[:, i].reshape(n_ctx, kv_lora).astype(BF16)], axis=0)
            kpe_all = jnp.concatenate(
                [o4f[:, kv_width:kv_width + HEAD],
                 jnp.pad(cache_c_pe[:, i].reshape(n_ctx, qk_rope), ((0, 0), (0, HEAD - qk_rope)))],
                axis=0)
            gk = jnp.pad(c_k_norm[i], (0, 2 * HEAD - qk_head))[None, :]
            t2 = t + n_ctx
            spb = dec_seq // TM_OUT
            k_c, v_c = pl.pallas_call(
                functools.partial(_kvb_kernel, tp // TM_OUT, t // TM_OUT, c_heads, qk_head),
                out_shape=(jax.ShapeDtypeStruct((t2, cw), BF16),
                           jax.ShapeDtypeStruct((t2, c_heads * v_dim), BF16)),
                grid=(t2 // TM_OUT,),
                in_specs=[pl.BlockSpec((TM_OUT, kv_lora), lambda m: (m, 0)),
                          pl.BlockSpec((TM_OUT, HEAD), lambda m: (m, 0)),
                          full((kv_lora, c_heads * (qk_nope + v_dim))), full((1, 2 * HEAD)),
                          pl.BlockSpec((TM_OUT, LANES), lambda m: (m % spb, 0)),
                          pl.BlockSpec((TM_OUT, LANES), lambda m: (m % spb, 0))],
                out_specs=(pl.BlockSpec((TM_OUT, cw), lambda m: (m, 0)),
                           pl.BlockSpec((TM_OUT, c_heads * v_dim), lambda m: (m, 0))),
                compiler_params=_cp(1),
                name="mla_kv_up",
            )(ckv_all, kpe_all, wkv, gk, cos_pe, sin_pe)

            scale_c = qk_head ** -0.5
            hp, hs = MLA_HEADS_PER_STEP[i % len(MLA_HEADS_PER_STEP)]
            oc = pl.pallas_call(
                functools.partial(_attn_prompt_kernel, TM // seq, seq, hp, 1, 2 * HEAD, scale_c,
                                  False),
                out_shape=jax.ShapeDtypeStruct((t, c_heads * v_dim), BF16),
                grid=(npb, c_heads // hp),
                in_specs=[pl.BlockSpec((TM, hp * 2 * HEAD), lambda m, j: (m, j)),
                          pl.BlockSpec((TM, hp * 2 * HEAD), lambda m, j: (m, j)),
                          pl.BlockSpec((TM, hp * v_dim), lambda m, j: (m, j))],
                out_specs=pl.BlockSpec((TM, hp * v_dim), lambda m, j: (m, j)),
                compiler_params=_cp(2),
                name="attn_c_prompt",
            )(q_c, k_c, v_c)
            cb0 = t // past
            oc = pl.pallas_call(
                functools.partial(_attn_sample_kernel, dec_seq, 256, hs, 1, 2 * HEAD, scale_c),
                out_shape=jax.ShapeDtypeStruct((t, c_heads * v_dim), BF16),
                grid=(dec_batch, c_heads // hs),
                in_specs=[pl.BlockSpec((TM, hs * 2 * HEAD), lambda b, j: (npb + b, j)),
                          pl.BlockSpec((TM, hs * 2 * HEAD), lambda b, j: (npb + b, j)),
                          pl.BlockSpec((TM, hs * v_dim), lambda b, j: (npb + b, j)),
                          pl.BlockSpec((past, hs * 2 * HEAD), lambda b, j: (cb0 + b, j)),
                          pl.BlockSpec((past, hs * v_dim), lambda b, j: (cb0 + b, j)),
                          pl.BlockSpec(memory_space=pl.ANY)],
                out_specs=pl.BlockSpec((TM, hs * v_dim), lambda b, j: (npb + b, j)),
                input_output_aliases={5: 0},
                compiler_params=_cp(2),
                name="attn_c_sample",
            )(q_c, k_c, v_c, k_c, v_c, oc)

            gw_ = groups * HEAD
            scale = HEAD ** -0.5
            sink = d_sink[i]
            kdb = kv_lora // HEAD
            od = pl.pallas_call(
                functools.partial(_attn_prompt_kernel, TM // seq, seq, 1, groups, HEAD, scale, True),
                out_shape=jax.ShapeDtypeStruct((t, d_width), BF16),
                grid_spec=pltpu.PrefetchScalarGridSpec(
                    num_scalar_prefetch=1, grid=(npb, kvh),
                    in_specs=[pl.BlockSpec((TM, gw_), lambda m, j, s: (m, j)),
                              pl.BlockSpec((TM, HEAD), lambda m, j, s: (m, kdb + j)),
                              pl.BlockSpec((TM, HEAD), lambda m, j, s: (m, j))],
                    out_specs=pl.BlockSpec((TM, gw_), lambda m, j, s: (m, j))),
                compiler_params=_cp(2),
                name="attn_d_prompt",
            )(sink, qd, o1b, o4b)
            ck = cache_d_k[:, i].reshape(dec_batch, past, kv_width)
            cv = cache_d_v[:, i].reshape(dec_batch, past, kv_width)
            od = pl.pallas_call(
                functools.partial(_attn_window_kernel, dec_seq, groups, scale),
                out_shape=jax.ShapeDtypeStruct((t, d_width), BF16),
                grid_spec=pltpu.PrefetchScalarGridSpec(
                    num_scalar_prefetch=1, grid=(dec_batch, kvh),
                    in_specs=[pl.BlockSpec((TM, gw_), lambda b, j, s: (npb + b, j)),
                              pl.BlockSpec((TM, HEAD), lambda b, j, s: (npb + b, kdb + j)),
                              pl.BlockSpec((TM, HEAD), lambda b, j, s: (npb + b, j)),
                              pl.BlockSpec((None, past, HEAD), lambda b, j, s: (b, 0, j)),
                              pl.BlockSpec((None, past, HEAD), lambda b, j, s: (b, 0, j)),
                              pl.BlockSpec(memory_space=pl.ANY)],
                    out_specs=pl.BlockSpec((TM, gw_), lambda b, j, s: (npb + b, j))),
                input_output_aliases={6: 0},
                compiler_params=_cp(2),
                name="attn_d_sample",
            )(sink, qd, o1b, o4b, ck, cv, od)
            lhs_a, lhs_b = oc, od

        wr = jnp.concatenate([w_group_router[l], w_expert_router[l]], axis=1)
        n_r = wr.shape[1]
        wr = jnp.pad(wr, ((0, 0), (0, LANES - n_r)))
        wr_hi = wr.astype(BF16)
        wr_cat = jnp.concatenate([wr_hi, (wr - wr_hi.astype(F32)).astype(BF16)], axis=1)
        rb = jnp.pad(jnp.concatenate([b_group_router[l], b_expert_router[l]]), (0, LANES - n_r))[None]
        n_out = d // TN_OUT
        ka = lhs_a.shape[1]
        x, h2, logits = pl.pallas_call(
            functools.partial(_out_kernel, n_out),
            out_shape=(jax.ShapeDtypeStruct((t, d), F32), jax.ShapeDtypeStruct((t, d), BF16),
                       jax.ShapeDtypeStruct((t, LANES), F32)),
            grid=(t // TM_OUT, n_out),
            in_specs=[pl.BlockSpec((TM_OUT, ka), lambda m, n: (m, 0)),
                      pl.BlockSpec((TM_OUT, lhs_b.shape[1]), lambda m, n: (m, 0)),
                      pl.BlockSpec((None, w_out.shape[1], TN_OUT), lambda m, n: (i, 0, n)),
                      pl.BlockSpec((TM_OUT, TN_OUT), lambda m, n: (m, n)),
                      mod_spec(2, TM_OUT, TN_OUT, 1),
                      full((1, d)), mod_spec(3, TM_OUT), mod_spec(4, TM_OUT),
                      full((d, 2 * LANES)), full((1, LANES))],
            out_specs=(pl.BlockSpec((TM_OUT, TN_OUT), lambda m, n: (m, n)),
                       pl.BlockSpec((TM_OUT, d), lambda m, n: (m, 0)),
                       pl.BlockSpec((TM_OUT, LANES), lambda m, n: (m, 0))),
            scratch_shapes=[pltpu.VMEM((n_out, TM_OUT, TN_OUT), F32)],
            compiler_params=_cp(2),
            name="out_proj",
        )(lhs_a, lhs_b, w_out, x, mt, norm2_g[l:l + 1], mt, mt, wr_cat, rb)

        lane_blk = pl.BlockSpec((TM_OUT, LANES), lambda m: (m, 0))
        gw, ridx, cnt = pl.pallas_call(
            _route_kernel,
            out_shape=(jax.ShapeDtypeStruct((t, LANES), F32),
                       jax.ShapeDtypeStruct((t, LANES), jnp.int32),
                       jax.ShapeDtypeStruct((8, LANES), F32)),
            grid=(t // TM_OUT,),
            in_specs=[lane_blk],
            out_specs=(lane_blk, lane_blk, full((8, LANES))),
            scratch_shapes=[pltpu.VMEM((8, LANES), F32)],
            compiler_params=_cp(1),
            name="route",
        )(logits)
        eid = ridx[:, :TOP_K].reshape(-1)
        rank = ridx[:, TOP_K:2 * TOP_K].reshape(-1)
        counts = cnt[0, :N_EXPERTS].astype(jnp.int32)
        n_pairs = t * TOP_K
        tb, gsz = EXPERT_TILING[l % len(EXPERT_TILING)]
        nblk_e = (counts + tb - 1) // tb
        blk_end = jnp.cumsum(nblk_e)
        blk_start = blk_end - nblk_e
        dest = (blk_start[eid] * tb + rank).astype(jnp.int32)
        n_blocks = (n_pairs + N_EXPERTS * (tb - 1) + tb - 1) // tb
        n_slots = n_blocks * tb
        slot_token = jnp.zeros((n_slots,), jnp.int32).at[dest].set(
            jnp.arange(n_pairs, dtype=jnp.int32) // TOP_K)
        blk_start = blk_start.astype(jnp.int32)
        nblk_e = nblk_e.astype(jnp.int32)
        xs = h2[slot_token]

        n_half = 2
        fc = d_expert // n_half
        any_spec = pl.BlockSpec(memory_space=pl.ANY)
        gt = gsz * tb
        act = pl.pallas_call(
            functools.partial(_up_kernel, tb, gsz),
            out_shape=jax.ShapeDtypeStruct((n_half, n_slots, fc), BF16),
            grid_spec=pltpu.PrefetchScalarGridSpec(
                num_scalar_prefetch=2, grid=(n_half, N_EXPERTS),
                in_specs=[any_spec,
                          pl.BlockSpec((None, None, d, fc), lambda cc, e, b0, nb: (l, e, 0, cc)),
                          pl.BlockSpec((None, None, d, fc), lambda cc, e, b0, nb: (l, e, 0, cc))],
                out_specs=any_spec,
                scratch_shapes=[pltpu.VMEM((d, fc), BF16), pltpu.VMEM((d, fc), BF16),
                                pltpu.VMEM((2, gt, d), BF16), pltpu.VMEM((2, gt, fc), BF16),
                                pltpu.SemaphoreType.DMA((2,)), pltpu.SemaphoreType.DMA((2,))]),
            compiler_params=_cp(2),
            name="expert_up",
        )(blk_start, nblk_e, xs, w_expert_gate, w_expert_up)
        dn = d // 2
        yb = pl.pallas_call(
            functools.partial(_down_kernel, tb, gsz),
            out_shape=jax.ShapeDtypeStruct((n_slots, d), F32),
            grid_spec=pltpu.PrefetchScalarGridSpec(
                num_scalar_prefetch=2, grid=(N_EXPERTS, d // dn),
                in_specs=[any_spec,
                          pl.BlockSpec((None, None, d_expert, dn),
                                       lambda e, hh, b0, nb: (l, e, 0, hh))],
                out_specs=any_spec,
                scratch_shapes=[pltpu.VMEM((d_expert, dn), BF16),
                                pltpu.VMEM((2, gt, d_expert), BF16), pltpu.VMEM((2, gt, dn), F32),
                                pltpu.SemaphoreType.DMA((2, n_half)), pltpu.SemaphoreType.DMA((2,))]),
            compiler_params=_cp(2),
            name="expert_down",
        )(blk_start, nblk_e, act, w_expert_down)

        n_cmb = t // TM_CMB
        dest2 = dest.reshape(n_cmb, TM_CMB, TOP_K)
        tbl = jnp.concatenate([dest2[:, :, k] for k in range(TOP_K)], axis=1)[:, None, :]
        with_next = l + 1 < depth
        npb_c = tp // TM_CMB
        row_blk = pl.BlockSpec((TM_CMB, d), lambda m: (m, 0))
        smem_blk = lambda fn: pl.BlockSpec((None, 1, TOP_K * TM_CMB), fn, memory_space=pltpu.SMEM)
        in_specs = [smem_blk(lambda m: (m, 0, 0)),
                    smem_blk(lambda m: (jnp.minimum(m + 1, n_cmb - 1), 0, 0)),
                    row_blk, any_spec,
                    pl.BlockSpec((TM_CMB, LANES), lambda m: (m, 0)), mod_spec(5, TM_CMB)]
        args = [tbl, tbl, x, yb, gw, mt]
        if with_next:
            in_specs += [full((1, d)), mod_spec(0, TM_CMB), mod_spec(1, TM_CMB)]
            args += [norm1_g[l + 1:l + 2], modtab[l + 1], modtab[l + 1]]
            out_shape = (jax.ShapeDtypeStruct((t, d), F32), jax.ShapeDtypeStruct((t, d), BF16))
            out_specs = (row_blk, row_blk)
        else:
            out_shape = (jax.ShapeDtypeStruct((tp, d), F32), jax.ShapeDtypeStruct((ts, d), F32))
            out_specs = (pl.BlockSpec((TM_CMB, d), lambda m: (jnp.minimum(m, npb_c - 1), 0)),
                         pl.BlockSpec((TM_CMB, d), lambda m: (jnp.maximum(m - npb_c, 0), 0)))
        res = pl.pallas_call(
            functools.partial(_combine_kernel, with_next, npb_c),
            out_shape=out_shape, grid=(n_cmb,), in_specs=in_specs, out_specs=out_specs,
            scratch_shapes=[pltpu.VMEM((2, TOP_K * TM_CMB, d), F32), pltpu.SemaphoreType.DMA((2,))],
            compiler_params=_cp(1),
            name="moe_combine",
        )(*args)
        if with_next:
            x, h = res

    y_prompt, y_sample = res
    return (y_prompt.reshape(batch, seq, d), y_sample.reshape(dec_batch, dec_seq, d),
            jnp.stack(a_k, axis=1), jnp.stack(a_v, axis=1), jnp.stack(c_kv, axis=1),
            jnp.stack(c_pe, axis=1), jnp.stack(d_k, axis=1), jnp.stack(d_v, axis=1))
```
